```python
import math
import jax, jax.numpy as jnp
from jax import lax
import numpy as np

D_MODEL = 1024
BATCH = 16
SEQ = 2048
DEPTH = 4

N_MIXERS = 2
N_CONV_LAYERS = (DEPTH + 1) // 2
N_SSM_LAYERS = DEPTH // 2
CONV_WIDTH = 3
SSM_GROUP = 16
SSM_GROUPS = D_MODEL // SSM_GROUP
SSM_STATE = 64
SSM_CHUNK = 128
DT_MIN = 1e-3
DT_MAX = 1e-1
EIG_CLIP = -1e-4
D_FF = 2816
MEM_LEN = 256
XA_HEADS = 4
XA_HEAD_DIM = D_MODEL // XA_HEADS
NORM_EPS = 1e-6
N_NORMS = 5

kernel_name = "hybrid_conv_s5_macaron_decoder"


def rmsnorm(x, g):
    xf = x.astype(jnp.float32)
    y = xf * lax.rsqrt(jnp.mean(xf * xf, axis=-1, keepdims=True) + NORM_EPS)
    return (y * g.astype(jnp.float32)).astype(x.dtype)


def swiglu_ffn(h, w_up, w_down):
    gate, up = jnp.split(h @ w_up, 2, axis=-1)
    return (jax.nn.silu(gate) * up) @ w_down


def short_conv_mixer(h, w_in, conv_w, w_out):
    c_gate, b_gate, v = jnp.split(h @ w_in, 3, axis=-1)
    u = c_gate * v
    conv = lax.conv_general_dilated(
        u, conv_w[:, None, :],
        window_strides=(1,),
        padding=[(CONV_WIDTH - 1, 0)],
        dimension_numbers=("NWC", "WIO", "NWC"),
        feature_group_count=D_MODEL)
    return (b_gate * conv) @ w_out


def _complex_scan_op(e1, e2):
    a1r, a1i, b1r, b1i = e1
    a2r, a2i, b2r, b2i = e2
    return (a2r * a1r - a2i * a1i,
            a2r * a1i + a2i * a1r,
            a2r * b1r - a2i * b1i + b2r,
            a2r * b1i + a2i * b1r + b2i)


def s5_mixer(h, a_re, a_im, log_dt, b_re, b_im, c_re, c_im, d_skip, w_glu):
    bsz, seq, _ = h.shape
    f32 = jnp.float32
    lam_re = jnp.minimum(a_re.astype(f32), EIG_CLIP)
    lam_im = a_im.astype(f32)
    dt = jnp.exp(log_dt.astype(f32))[:, None]
    mag = jnp.exp(lam_re * dt)
    abar_re = mag * jnp.cos(lam_im * dt)
    abar_im = mag * jnp.sin(lam_im * dt)
    den = lam_re * lam_re + lam_im * lam_im
    num_re = abar_re - 1.0
    num_im = abar_im
    coef_re = (num_re * lam_re + num_im * lam_im) / den
    coef_im = (num_im * lam_re - num_re * lam_im) / den
    br = b_re.astype(f32)
    bi = b_im.astype(f32)
    bbar_re = coef_re[..., None] * br - coef_im[..., None] * bi
    bbar_im = coef_re[..., None] * bi + coef_im[..., None] * br
    cr = c_re.astype(f32)
    ci = c_im.astype(f32)

    n_chunks = seq // SSM_CHUNK
    u = h.astype(f32).reshape(bsz, n_chunks, SSM_CHUNK, SSM_GROUPS, SSM_GROUP)
    u = jnp.moveaxis(u, 1, 0)
    el_shape = (bsz, SSM_CHUNK, SSM_GROUPS, SSM_STATE)
    a_el_re = jnp.broadcast_to(abar_re, el_shape)
    a_el_im = jnp.broadcast_to(abar_im, el_shape)

    def chunk_step(carry, u_c):
        h_re, h_im = carry
        bu_re = jnp.einsum("btgh,gph->btgp", u_c, bbar_re)
        bu_im = jnp.einsum("btgh,gph->btgp", u_c, bbar_im)
        acc_re, acc_im, loc_re, loc_im = lax.associative_scan(
            _complex_scan_op, (a_el_re, a_el_im, bu_re, bu_im), axis=1)
        s_re = loc_re + acc_re * h_re[:, None] - acc_im * h_im[:, None]
        s_im = loc_im + acc_re * h_im[:, None] + acc_im * h_re[:, None]
        y_c = (jnp.einsum("btgp,ghp->btgh", s_re, cr)
               - jnp.einsum("btgp,ghp->btgh", s_im, ci))
        return (s_re[:, -1], s_im[:, -1]), y_c

    h0 = (jnp.zeros((bsz, SSM_GROUPS, SSM_STATE), f32),
          jnp.zeros((bsz, SSM_GROUPS, SSM_STATE), f32))
    _, y = lax.scan(chunk_step, h0, u)
    y = jnp.moveaxis(y, 0, 1).reshape(bsz, seq, D_MODEL)
    y = y + d_skip.astype(f32) * h.astype(f32)
    z = jax.nn.gelu(y).astype(h.dtype)
    val, gate = jnp.split(z @ w_glu, 2, axis=-1)
    return val * jax.nn.sigmoid(gate)


def memory_cross_attention(h, mem_n, w_q, w_kv, w_o):
    bsz, seq, _ = h.shape
    mlen = mem_n.shape[1]
    q = (h @ w_q).reshape(bsz, seq, XA_HEADS, XA_HEAD_DIM)
    k, v = jnp.split(mem_n @ w_kv, 2, axis=-1)
    k = k.reshape(bsz, mlen, XA_HEADS, XA_HEAD_DIM)
    v = v.reshape(bsz, mlen, XA_HEADS, XA_HEAD_DIM)
    s = jnp.einsum("bqhd,bkhd->bhqk", q, k).astype(jnp.float32) * (XA_HEAD_DIM ** -0.5)
    p = jax.nn.softmax(s, axis=-1).astype(v.dtype)
    o = jnp.einsum("bhqk,bkhd->bqhd", p, v).reshape(bsz, seq, D_MODEL)
    return o @ w_o


def setup_inputs(seed: int = 0) -> dict:
    key = jax.random.key(seed)
    ks = jax.random.split(key, 24)
    D = D_MODEL
    nrm = jax.random.normal

    def w(k, shape, fan_in):
        return nrm(k, shape, jnp.float32) * (fan_in ** -0.5)

    x = nrm(ks[0], (BATCH, SEQ, D), jnp.float32)
    mem = nrm(ks[1], (BATCH, MEM_LEN, D), jnp.float32)
    norm_g = 1.0 + 0.02 * nrm(ks[2], (DEPTH, N_NORMS, D), jnp.float32)
    final_g = 1.0 + 0.02 * nrm(ks[3], (D,), jnp.float32)
    ffn1_up = w(ks[4], (DEPTH, D, 2 * D_FF), D)
    ffn1_down = w(ks[5], (DEPTH, D_FF, D), D_FF)
    ffn2_up = w(ks[6], (DEPTH, D, 2 * D_FF), D)
    ffn2_down = w(ks[7], (DEPTH, D_FF, D), D_FF)
    conv_w_in = w(ks[8], (N_CONV_LAYERS, D, 3 * D), D)
    conv_w = w(ks[9], (N_CONV_LAYERS, CONV_WIDTH, D), CONV_WIDTH)
    conv_w_out = w(ks[10], (N_CONV_LAYERS, D, D), D)
    ssm_shape = (N_SSM_LAYERS, SSM_GROUPS, SSM_STATE)
    ssm_a_re = -0.5 + 0.01 * nrm(ks[11], ssm_shape, jnp.float32)
    ssm_a_im = (math.pi * jnp.arange(SSM_STATE, dtype=jnp.float32)[None, None, :]
                + 0.01 * nrm(ks[12], ssm_shape, jnp.float32))
    ssm_log_dt = jax.random.uniform(ks[13], (N_SSM_LAYERS, SSM_GROUPS), jnp.float32,
                                    math.log(DT_MIN), math.log(DT_MAX))
    ssm_b_re = w(ks[14], (N_SSM_LAYERS, SSM_GROUPS, SSM_STATE, SSM_GROUP), 2 * SSM_GROUP)
    ssm_b_im = w(ks[15], (N_SSM_LAYERS, SSM_GROUPS, SSM_STATE, SSM_GROUP), 2 * SSM_GROUP)
    ssm_c_re = w(ks[16], (N_SSM_LAYERS, SSM_GROUPS, SSM_GROUP, SSM_STATE), SSM_STATE)
    ssm_c_im = w(ks[17], (N_SSM_LAYERS, SSM_GROUPS, SSM_GROUP, SSM_STATE), SSM_STATE)
    ssm_d = nrm(ks[18], (N_SSM_LAYERS, D), jnp.float32)
    ssm_w_glu = w(ks[19], (N_SSM_LAYERS, D, 2 * D), D)
    xa_w_q = w(ks[20], (DEPTH, D, D), D)
    xa_w_kv = w(ks[21], (DEPTH, D, 2 * D), D)
    xa_w_o = w(ks[22], (DEPTH, D, D), D)
    return {"x": x, "mem": mem, "norm_g": norm_g, "final_g": final_g,
            "ffn1_up": ffn1_up, "ffn1_down": ffn1_down,
            "ffn2_up": ffn2_up, "ffn2_down": ffn2_down,
            "conv_w_in": conv_w_in, "conv_w": conv_w, "conv_w_out": conv_w_out,
            "ssm_a_re": ssm_a_re, "ssm_a_im": ssm_a_im, "ssm_log_dt": ssm_log_dt,
            "ssm_b_re": ssm_b_re, "ssm_b_im": ssm_b_im,
            "ssm_c_re": ssm_c_re, "ssm_c_im": ssm_c_im,
            "ssm_d": ssm_d, "ssm_w_glu": ssm_w_glu,
            "xa_w_q": xa_w_q, "xa_w_kv": xa_w_kv, "xa_w_o": xa_w_o}


def reference(x, mem, norm_g, final_g, ffn1_up, ffn1_down, ffn2_up, ffn2_down,
              conv_w_in, conv_w, conv_w_out,
              ssm_a_re, ssm_a_im, ssm_log_dt, ssm_b_re, ssm_b_im,
              ssm_c_re, ssm_c_im, ssm_d, ssm_w_glu,
              xa_w_q, xa_w_kv, xa_w_o):
    for i in range(DEPTH):
        g = norm_g[i]
        x = x + 0.5 * swiglu_ffn(rmsnorm(x, g[0]), ffn1_up[i], ffn1_down[i])
        h = rmsnorm(x, g[1])
        j = i // N_MIXERS
        if i % N_MIXERS == 0:
            x = x + short_conv_mixer(h, conv_w_in[j], conv_w[j], conv_w_out[j])
        else:
            x = x + s5_mixer(h, ssm_a_re[j], ssm_a_im[j], ssm_log_dt[j],
                             ssm_b_re[j], ssm_b_im[j], ssm_c_re[j], ssm_c_im[j],
                             ssm_d[j], ssm_w_glu[j])
        x = x + memory_cross_attention(rmsnorm(x, g[2]), rmsnorm(mem, g[3]),
                                       xa_w_q[i], xa_w_kv[i], xa_w_o[i])
        x = x + 0.5 * swiglu_ffn(rmsnorm(x, g[4]), ffn2_up[i], ffn2_down[i])
    return rmsnorm(x, final_g)
```

```python
import functools
import math

import jax
import jax.numpy as jnp
from jax import lax
from jax.experimental import pallas as pl
from jax.experimental.pallas import tpu as pltpu

F32 = jnp.float32
BF16 = jnp.bfloat16

NORM_EPS = 1e-6
EIG_CLIP = -1e-4
N_NORMS = 5
CONV_WIDTH = 3
SSM_GROUP = 16
SSM_STATE = 64
XA_HEADS = 4

LANES = 128
SUBLANES = 8
V7X_VMEM_BYTES = 64 * 1024 * 1024

GROUPS_PER_BLOCK = LANES // SSM_GROUP
PAIRS_PER_BLOCK = GROUPS_PER_BLOCK // 2
STATE_COLS = GROUPS_PER_BLOCK * 2 * SSM_STATE


def _rms(x, g):
    ms = jnp.mean(x * x, axis=-1, keepdims=True)
    return x * lax.rsqrt(ms + NORM_EPS) * g


def _dot(a, b):
    return jnp.dot(a, b, preferred_element_type=F32)


def _resident(block_shape, index_map):
    return pl.BlockSpec(block_shape, index_map, pipeline_mode=pl.Buffered(1))


def _params(semantics, vmem_bytes):
    return pltpu.CompilerParams(dimension_semantics=semantics,
                                vmem_limit_bytes=min(vmem_bytes, V7X_VMEM_BYTES))


FFN_ROWS = 512
FFN_CHUNK = 256


def _ffn_kernel(x_ref, g_ref, wup_ref, wdn_ref, fg_ref, o_ref, act_ref, *, final_norm):
    d_ff = wdn_ref.shape[0]
    x = x_ref[...]
    h = _rms(x, g_ref[...]).astype(BF16)
    for c in range(d_ff // FFN_CHUNK):
        lo = c * FFN_CHUNK
        gate = _dot(h, wup_ref[:, lo:lo + FFN_CHUNK])
        up = _dot(h, wup_ref[:, d_ff + lo:d_ff + lo + FFN_CHUNK])
        act_ref[:, lo:lo + FFN_CHUNK] = (jax.nn.silu(gate) * up).astype(BF16)
    y = x + 0.5 * _dot(act_ref[...], wdn_ref[...])
    if final_norm:
        y = _rms(y, fg_ref[...])
    o_ref[...] = y


def _ffn(x2, norm_rows, g_row, w_up, w_dn, layer, final_g, final_norm):
    n, d = x2.shape
    d_ff = w_dn.shape[1]
    assert n % FFN_ROWS == 0 and d_ff % FFN_CHUNK == 0
    vmem = (4 * FFN_ROWS * d * 4
            + (d * 2 * d_ff + d_ff * d) * 2
            + FFN_ROWS * d_ff * 2
            + 8 * FFN_ROWS * d * 4)
    return pl.pallas_call(
        functools.partial(_ffn_kernel, final_norm=final_norm),
        out_shape=jax.ShapeDtypeStruct((n, d), F32),
        grid=(n // FFN_ROWS,),
        in_specs=[
            pl.BlockSpec((FFN_ROWS, d), lambda i: (i, 0)),
            pl.BlockSpec((None, 1, d), lambda i: (g_row, 0, 0)),
            _resident((None, d, 2 * d_ff), lambda i: (layer, 0, 0)),
            _resident((None, d_ff, d), lambda i: (layer, 0, 0)),
            pl.BlockSpec((1, d), lambda i: (0, 0)),
        ],
        out_specs=pl.BlockSpec((FFN_ROWS, d), lambda i: (i, 0)),
        scratch_shapes=[pltpu.VMEM((FFN_ROWS, d_ff), BF16)],
        compiler_params=_params(("parallel",), vmem),
        name="ffn_final" if final_norm else "ffn",
    )(x2, norm_rows, w_up, w_dn, final_g)


CONV_ROWS = 512


def _conv_kernel(x_ref, g_ref, win_ref, cw_ref, wout_ref, o_ref, u_ref):
    rows, d = x_ref.shape
    x = x_ref[...]
    h = _rms(x, g_ref[...]).astype(BF16)
    cbv = _dot(h, win_ref[...])
    u = cbv[:, :d] * cbv[:, 2 * d:]
    b_gate = cbv[:, d:2 * d]

    @pl.when(pl.program_id(1) == 0)
    def _():
        u_ref[0:SUBLANES, :] = jnp.zeros((SUBLANES, d), F32)

    @pl.when(pl.program_id(1) > 0)
    def _():
        u_ref[0:SUBLANES, :] = u_ref[rows:rows + SUBLANES, :]

    u_ref[SUBLANES:SUBLANES + rows, :] = u
    w = cw_ref[...]
    conv = w[CONV_WIDTH - 1:CONV_WIDTH, :] * u
    for k in range(CONV_WIDTH - 1):
        shift = CONV_WIDTH - 1 - k
        conv = conv + w[k:k + 1, :] * u_ref[SUBLANES - shift:SUBLANES - shift + rows, :]
    y = (b_gate * conv).astype(BF16)
    o_ref[...] = x + _dot(y, wout_ref[...])


def _conv_mixer(x2, batch, norm_rows, g_row, w_in, conv_w, w_out, j):
    n, d = x2.shape
    seq = n // batch
    assert seq % CONV_ROWS == 0
    nl = seq // CONV_ROWS
    vmem = (4 * CONV_ROWS * d * 4 + (3 * d * d + d * d) * 2
            + (CONV_ROWS + SUBLANES) * d * 4 + 8 * CONV_ROWS * d * 4)
    return pl.pallas_call(
        _conv_kernel,
        out_shape=jax.ShapeDtypeStruct((n, d), F32),
        grid=(batch, nl),
        in_specs=[
            pl.BlockSpec((CONV_ROWS, d), lambda b, l: (b * nl + l, 0)),
            pl.BlockSpec((None, 1, d), lambda b, l: (g_row, 0, 0)),
            _resident((None, d, 3 * d), lambda b, l: (j, 0, 0)),
            pl.BlockSpec((None, CONV_WIDTH, d), lambda b, l: (j, 0, 0)),
            _resident((None, d, d), lambda b, l: (j, 0, 0)),
        ],
        out_specs=pl.BlockSpec((CONV_ROWS, d), lambda b, l: (b * nl + l, 0)),
        scratch_shapes=[pltpu.VMEM((CONV_ROWS + SUBLANES, d), F32)],
        compiler_params=_params(("arbitrary", "arbitrary"), vmem),
        name="conv_mixer",
    )(x2, norm_rows, w_in, conv_w, w_out)


XA_ROWS = 512


def _kv_kernel(m_ref, g_ref, wkv_ref, kt_ref, v_ref):
    d = m_ref.shape[1]
    mn = _rms(m_ref[...], g_ref[...]).astype(BF16)
    kv = _dot(mn, wkv_ref[...])
    kt_ref[...] = kv[:, :d].T.astype(BF16)
    v_ref[...] = kv[:, d:].astype(BF16)


def _memory_kv(mem, norm_rows, g_row, w_kv, layer):
    batch, mlen, d = mem.shape
    vmem = 2 * mlen * d * 4 + 2 * d * 2 * d * 2 + 4 * mlen * d * 2 + 8 * mlen * d * 4
    return pl.pallas_call(
        _kv_kernel,
        out_shape=(jax.ShapeDtypeStruct((batch, d, mlen), BF16),
                   jax.ShapeDtypeStruct((batch, mlen, d), BF16)),
        grid=(batch,),
        in_specs=[
            pl.BlockSpec((None, mlen, d), lambda b: (b, 0, 0)),
            pl.BlockSpec((None, 1, d), lambda b: (g_row, 0, 0)),
            _resident((None, d, 2 * d), lambda b: (layer, 0, 0)),
        ],
        out_specs=(pl.BlockSpec((None, d, mlen), lambda b: (b, 0, 0)),
                   pl.BlockSpec((None, mlen, d), lambda b: (b, 0, 0))),
        compiler_params=_params(("parallel",), vmem),
        name="memory_kv",
    )(mem, norm_rows, w_kv)


def _xattn_kernel(x_ref, g_ref, wq_ref, kt_ref, v_ref, wo_ref, o_ref, cat_ref):
    d = x_ref.shape[1]
    hd = d // XA_HEADS
    scale = hd ** -0.5
    x = x_ref[...]
    h = _rms(x, g_ref[...]).astype(BF16)
    q = _dot(h, wq_ref[...]).astype(BF16)
    for a in range(XA_HEADS):
        lo = a * hd
        s = _dot(q[:, lo:lo + hd], kt_ref[lo:lo + hd, :]) * scale
        e = jnp.exp(s - jnp.max(s, axis=-1, keepdims=True))
        p = (e / jnp.sum(e, axis=-1, keepdims=True)).astype(BF16)
        cat_ref[:, lo:lo + hd] = _dot(p, v_ref[:, lo:lo + hd]).astype(BF16)
    o_ref[...] = x + _dot(cat_ref[...], wo_ref[...])


def _cross_attention(x2, batch, norm_rows, g_row, w_q, kt, v, w_o, layer):
    n, d = x2.shape
    seq = n // batch
    mlen = v.shape[1]
    assert seq % XA_ROWS == 0
    nl = seq // XA_ROWS
    vmem = (4 * XA_ROWS * d * 4 + 2 * d * d * 2 + 4 * mlen * d * 2
            + XA_ROWS * d * 2 + 8 * XA_ROWS * d * 4)
    return pl.pallas_call(
        _xattn_kernel,
        out_shape=jax.ShapeDtypeStruct((n, d), F32),
        grid=(batch, nl),
        in_specs=[
            pl.BlockSpec((XA_ROWS, d), lambda b, l: (b * nl + l, 0)),
            pl.BlockSpec((None, 1, d), lambda b, l: (g_row, 0, 0)),
            _resident((None, d, d), lambda b, l: (layer, 0, 0)),
            pl.BlockSpec((None, d, mlen), lambda b, l: (b, 0, 0)),
            pl.BlockSpec((None, mlen, d), lambda b, l: (b, 0, 0)),
            _resident((None, d, d), lambda b, l: (layer, 0, 0)),
        ],
        out_specs=pl.BlockSpec((XA_ROWS, d), lambda b, l: (b * nl + l, 0)),
        scratch_shapes=[pltpu.VMEM((XA_ROWS, d), BF16)],
        compiler_params=_params(("parallel", "parallel"), vmem),
        name="cross_attention",
    )(x2, norm_rows, w_q, kt, v, w_o)


S5_STEPS = 32


def _s5_disc_kernel(are_ref, aim_ref, ldt_ref, arex_ref, aimx_ref, ldtx_ref,
                    bre_ref, bim_ref, abr_ref, abi_ref, bbr_ref, bbi_ref):
    def abar(a_re, a_im, log_dt):
        lam_re = jnp.minimum(a_re, EIG_CLIP)
        dt = jnp.exp(log_dt)
        mag = jnp.exp(lam_re * dt)
        return lam_re, a_im, mag * jnp.cos(a_im * dt), mag * jnp.sin(a_im * dt)

    _, _, ab_re, ab_im = abar(are_ref[...], aim_ref[...], ldt_ref[...])
    abr_ref[...] = ab_re
    abi_ref[...] = ab_im
    lam_re, lam_im, ab_re, ab_im = abar(arex_ref[...], aimx_ref[...], ldtx_ref[...])
    den = lam_re * lam_re + lam_im * lam_im
    num_re = ab_re - 1.0
    num_im = ab_im
    coef_re = (num_re * lam_re + num_im * lam_im) / den
    coef_im = (num_im * lam_re - num_re * lam_im) / den
    br = bre_ref[...]
    bi = bim_ref[...]
    bbr_ref[...] = coef_re * br - coef_im * bi
    bbi_ref[...] = coef_re * bi + coef_im * br


def _s5_discretise(a_re, a_im, log_dt, b_re, b_im):
    nl, g, p = a_re.shape
    h = b_re.shape[-1]
    ldt = jnp.broadcast_to(log_dt[:, :, None], (nl, g, p))
    rep = lambda a: jnp.repeat(a, h, axis=-1)
    small = pl.BlockSpec((None, g, p), lambda i: (i, 0, 0))
    big = pl.BlockSpec((None, g, p * h), lambda i: (i, 0, 0))
    return pl.pallas_call(
        _s5_disc_kernel,
        out_shape=(jax.ShapeDtypeStruct((nl, g, p), F32),) * 2
        + (jax.ShapeDtypeStruct((nl, g, p * h), F32),) * 2,
        grid=(nl,),
        in_specs=[small, small, small, big, big, big, big, big],
        out_specs=(small, small, big, big),
        compiler_params=_params(("parallel",), 16 * g * p * h * 4 * 4),
        name="s5_discretise",
    )(a_re, a_im, ldt, rep(a_re), rep(a_im), rep(ldt),
      b_re.reshape(nl, g, p * h), b_im.reshape(nl, g, p * h))


def _s5_pack(abar_re, abar_im, bbar_re, bbar_im, c_re, c_im):
    nl, g, p = abar_re.shape
    h = SSM_GROUP
    nblk = g // GROUPS_PER_BLOCK
    eye_q = jnp.eye(PAIRS_PER_BLOCK, dtype=F32)
    eye_g = jnp.eye(2, dtype=F32)
    split = lambda a: a.reshape((nl, nblk, PAIRS_PER_BLOCK, 2) + a.shape[2:])
    b = jnp.stack([split(bbar_re.reshape(nl, g, p, h)),
                   split(bbar_im.reshape(nl, g, p, h))], axis=1)
    wb = jnp.einsum("lcjqgph,qr,gs->ljrshqcgp", b, eye_q, eye_g)
    wb = wb.reshape(nl, nblk, LANES, STATE_COLS).astype(BF16)
    c = jnp.stack([split(c_re), split(-c_im)], axis=1)
    wc = jnp.einsum("lcjqghp,qr,gs->ljqcgprsh", c, eye_q, eye_g)
    wc = wc.reshape(nl, nblk, STATE_COLS, LANES).astype(BF16)
    a = jnp.stack([abar_re, abar_im], axis=1)
    a = a.reshape(nl, 2, nblk, PAIRS_PER_BLOCK, 2 * p)
    a = a.transpose(0, 2, 3, 1, 4).reshape(nl, nblk, 2 * PAIRS_PER_BLOCK, 2 * p)
    return wb, wc, a


def _s5_kernel(x_ref, g_ref, wb_ref, wc_ref, a_ref, dskip_ref, wglu_ref, o_ref,
               hn_ref, hnt_ref, bu_ref, st_ref, yt_ref, ybm_ref, z_ref, carry_ref):
    batch, steps, d = x_ref.shape
    rows = batch * steps
    nblk = d // LANES

    @pl.when(pl.program_id(0) == 0)
    def _():
        carry_ref[...] = jnp.zeros(carry_ref.shape, F32)

    x = x_ref[...].reshape(rows, d)
    hn = _rms(x, g_ref[...])
    for j in range(nblk):
        hn_ref[j] = hn[:, j * LANES:(j + 1) * LANES]

    for j in range(nblk):
        for t in range(steps):
            hnt_ref[t * batch:(t + 1) * batch, :] = (
                hn_ref[j, pl.ds(t, batch, stride=steps), :].astype(BF16))
        bu_ref[...] = _dot(hnt_ref[...], wb_ref[j])

        a = a_ref[j]
        coef = [jnp.broadcast_to(a[r:r + 1, :], (batch, LANES))
                for r in range(2 * PAIRS_PER_BLOCK)]

        def step(t, state):
            r0 = pl.multiple_of(t * batch, batch)
            new = []
            for q in range(PAIRS_PER_BLOCK):
                s_re, s_im = state[2 * q], state[2 * q + 1]
                a_re, a_im = coef[2 * q], coef[2 * q + 1]
                c_re = q * 2 * LANES
                c_im = c_re + LANES
                n_re = a_re * s_re - a_im * s_im + bu_ref[pl.ds(r0, batch), c_re:c_re + LANES]
                n_im = a_re * s_im + a_im * s_re + bu_ref[pl.ds(r0, batch), c_im:c_im + LANES]
                st_ref[pl.ds(r0, batch), c_re:c_re + LANES] = n_re.astype(BF16)
                st_ref[pl.ds(r0, batch), c_im:c_im + LANES] = n_im.astype(BF16)
                new += [n_re, n_im]
            return tuple(new)

        state = lax.fori_loop(
            0, steps, step,
            tuple(carry_ref[j, r] for r in range(2 * PAIRS_PER_BLOCK)), unroll=4)
        for r in range(2 * PAIRS_PER_BLOCK):
            carry_ref[j, r] = state[r]

        yt_ref[...] = _dot(st_ref[...], wc_ref[j])
        for t in range(steps):
            ybm_ref[j, pl.ds(t, batch, stride=steps), :] = yt_ref[t * batch:(t + 1) * batch, :]

    for j in range(nblk):
        y = ybm_ref[j] + dskip_ref[:, j * LANES:(j + 1) * LANES] * hn_ref[j]
        z_ref[:, j * LANES:(j + 1) * LANES] = jax.nn.gelu(y).astype(BF16)
    glu = _dot(z_ref[...], wglu_ref[...])
    out = x_ref[...].reshape(rows, d) + glu[:, :d] * jax.nn.sigmoid(glu[:, d:])
    o_ref[...] = out.reshape(batch, steps, d)


def _s5_mixer(x3, norm_rows, g_row, wb, wc, a, d_skip, w_glu, j):
    batch, seq, d = x3.shape
    steps = S5_STEPS
    assert seq % steps == 0 and steps % SUBLANES == 0 and batch % (2 * SUBLANES) == 0
    rows = batch * steps
    nblk = d // LANES
    vmem = (4 * rows * d * 4
            + 2 * 2 * nblk * LANES * STATE_COLS * 2
            + 2 * d * d * 2
            + 2 * rows * d * 4 + rows * STATE_COLS * 6 + rows * d * 2
            + 8 * rows * d * 4)
    return pl.pallas_call(
        _s5_kernel,
        out_shape=jax.ShapeDtypeStruct((batch, seq, d), F32),
        grid=(seq // steps,),
        in_specs=[
            pl.BlockSpec((batch, steps, d), lambda c: (0, c, 0)),
            pl.BlockSpec((None, 1, d), lambda c: (g_row, 0, 0)),
            _resident((None, nblk, LANES, STATE_COLS), lambda c: (j, 0, 0, 0)),
            _resident((None, nblk, STATE_COLS, LANES), lambda c: (j, 0, 0, 0)),
            _resident((None, nblk, 2 * PAIRS_PER_BLOCK, LANES), lambda c: (j, 0, 0, 0)),
            pl.BlockSpec((None, 1, d), lambda c: (j, 0, 0)),
            _resident((None, d, 2 * d), lambda c: (j, 0, 0)),
        ],
        out_specs=pl.BlockSpec((batch, steps, d), lambda c: (0, c, 0)),
        scratch_shapes=[
            pltpu.VMEM((nblk, rows, LANES), F32),
            pltpu.VMEM((rows, LANES), BF16),
            pltpu.VMEM((rows, STATE_COLS), F32),
            pltpu.VMEM((rows, STATE_COLS), BF16),
            pltpu.VMEM((rows, LANES), F32),
            pltpu.VMEM((nblk, rows, LANES), F32),
            pltpu.VMEM((rows, d), BF16),
            pltpu.VMEM((nblk, 2 * PAIRS_PER_BLOCK, batch, LANES), F32),
        ],
        compiler_params=_params(("arbitrary",), vmem),
        name="s5_mixer",
    )(x3, norm_rows, wb, wc, a, d_skip, w_glu)


def kernel(x, mem, norm_g, final_g, ffn1_up, ffn1_down, ffn2_up, ffn2_down,
           conv_w_in, conv_w, conv_w_out,
           ssm_a_re, ssm_a_im, ssm_log_dt, ssm_b_re, ssm_b_im,
           ssm_c_re, ssm_c_im, ssm_d, ssm_w_glu,
           xa_w_q, xa_w_kv, xa_w_o):
    batch, seq, d = x.shape
    depth = norm_g.shape[0]
    assert norm_g.shape[1] == N_NORMS
    assert ssm_b_re.shape[-1] == SSM_GROUP and ssm_a_re.shape[-1] == SSM_STATE

    norm_rows = norm_g.reshape(depth * N_NORMS, 1, d)
    final_row = final_g.reshape(1, d)
    bf = lambda w: w.astype(BF16)
    ffn1_up, ffn1_down, ffn2_up, ffn2_down = map(bf, (ffn1_up, ffn1_down, ffn2_up, ffn2_down))
    conv_w_in, conv_w_out, ssm_w_glu = map(bf, (conv_w_in, conv_w_out, ssm_w_glu))
    xa_w_q, xa_w_kv, xa_w_o = map(bf, (xa_w_q, xa_w_kv, xa_w_o))

    abar_re, abar_im, bbar_re, bbar_im = _s5_discretise(
        ssm_a_re, ssm_a_im, ssm_log_dt, ssm_b_re, ssm_b_im)
    s5_wb, s5_wc, s5_a = _s5_pack(abar_re, abar_im, bbar_re, bbar_im, ssm_c_re, ssm_c_im)
    s5_d = ssm_d.reshape(ssm_d.shape[0], 1, d)

    x2 = x.reshape(batch * seq, d)
    for i in range(depth):
        row = i * N_NORMS
        j = i // 2
        x2 = _ffn(x2, norm_rows, row + 0, ffn1_up, ffn1_down, i, final_row, False)
        if i % 2 == 0:
            x2 = _conv_mixer(x2, batch, norm_rows, row + 1, conv_w_in, conv_w, conv_w_out, j)
        else:
            x2 = _s5_mixer(x2.reshape(batch, seq, d), norm_rows, row + 1,
                           s5_wb, s5_wc, s5_a, s5_d, ssm_w_glu, j).reshape(batch * seq, d)
        kt, v = _memory_kv(mem, norm_rows, row + 3, xa_w_kv, i)
        x2 = _cross_attention(x2, batch, norm_rows, row + 2, xa_w_q, kt, v, xa_w_o, i)
        x2 = _ffn(x2, norm_rows, row + 4, ffn2_up, ffn2_down, i, final_row, i == depth - 1)
    return x2.reshape(batch, seq, d)
```

```python
import functools
import math

import jax
import jax.numpy as jnp
from jax import lax
from jax.experimental import pallas as pl
from jax.experimental.pallas import tpu as pltpu

F32 = jnp.float32
BF16 = jnp.bfloat16

NORM_EPS = 1e-6
EIG_CLIP = -1e-4
N_NORMS = 5
CONV_WIDTH = 3
SSM_GROUP = 16
SSM_STATE = 64
XA_HEADS = 4

LANES = 128
SUBLANES = 8
V7X_VMEM_BYTES = 64 * 1024 * 1024

GROUPS_PER_BLOCK = LANES // SSM_GROUP
PAIRS_PER_BLOCK = GROUPS_PER_BLOCK // 2
STATE_COLS = GROUPS_PER_BLOCK * 2 * SSM_STATE


def _rms(x, g):
    ms = jnp.mean(x * x, axis=-1, keepdims=True)
    return x * lax.rsqrt(ms + NORM_EPS) * g


def _dot(a, b):
    return jnp.dot(a, b, preferred_element_type=F32)


def _resident(block_shape, index_map):
    return pl.BlockSpec(block_shape, index_map, pipeline_mode=pl.Buffered(1))


def _params(semantics, vmem_bytes):
    return pltpu.CompilerParams(dimension_semantics=semantics,
                                vmem_limit_bytes=min(vmem_bytes, V7X_VMEM_BYTES))


FFN_ROWS = 512
FFN_CHUNK = 256


def _ffn_kernel(x_ref, g_ref, wup_ref, wdn_ref, fg_ref, o_ref, act_ref, *, final_norm):
    d_ff = wdn_ref.shape[0]
    x = x_ref[...]
    h = _rms(x, g_ref[...]).astype(BF16)
    for c in range(d_ff // FFN_CHUNK):
        lo = c * FFN_CHUNK
        gate = _dot(h, wup_ref[:, lo:lo + FFN_CHUNK])
        up = _dot(h, wup_ref[:, d_ff + lo:d_ff + lo + FFN_CHUNK])
        act_ref[:, lo:lo + FFN_CHUNK] = (jax.nn.silu(gate) * up).astype(BF16)
    y = x + 0.5 * _dot(act_ref[...], wdn_ref[...])
    if final_norm:
        y = _rms(y, fg_ref[...])
    o_ref[...] = y


def _ffn(x3, norm_rows, g_row, w_up, w_dn, layer, final_g, final_norm):
    batch, seq, d = x3.shape
    d_ff = w_dn.shape[1]
    assert seq % FFN_ROWS == 0 and d_ff % FFN_CHUNK == 0
    row_tile = pl.BlockSpec((None, FFN_ROWS, d), lambda b, l: (b, l, 0))
    vmem = (4 * FFN_ROWS * d * 4
            + (d * 2 * d_ff + d_ff * d) * 2
            + FFN_ROWS * d_ff * 2
            + 8 * FFN_ROWS * d * 4)
    return pl.pallas_call(
        functools.partial(_ffn_kernel, final_norm=final_norm),
        out_shape=jax.ShapeDtypeStruct((batch, seq, d), F32),
        grid=(batch, seq // FFN_ROWS),
        in_specs=[
            row_tile,
            pl.BlockSpec((None, 1, d), lambda b, l: (g_row, 0, 0)),
            _resident((None, d, 2 * d_ff), lambda b, l: (layer, 0, 0)),
            _resident((None, d_ff, d), lambda b, l: (layer, 0, 0)),
            pl.BlockSpec((1, d), lambda b, l: (0, 0)),
        ],
        out_specs=row_tile,
        scratch_shapes=[pltpu.VMEM((FFN_ROWS, d_ff), BF16)],
        compiler_params=_params(("parallel", "parallel"), vmem),
        name="ffn_final" if final_norm else "ffn",
    )(x3, norm_rows, w_up, w_dn, final_g)


CONV_ROWS = 1024


def _conv_kernel(x_ref, g_ref, win_ref, cw_ref, wout_ref, o_ref, u_ref):
    rows, d = x_ref.shape
    x = x_ref[...]
    h = _rms(x, g_ref[...]).astype(BF16)
    cbv = _dot(h, win_ref[...])
    u = cbv[:, :d] * cbv[:, 2 * d:]
    b_gate = cbv[:, d:2 * d]

    @pl.when(pl.program_id(1) == 0)
    def _():
        u_ref[0:SUBLANES, :] = jnp.zeros((SUBLANES, d), F32)

    @pl.when(pl.program_id(1) > 0)
    def _():
        u_ref[0:SUBLANES, :] = u_ref[rows:rows + SUBLANES, :]

    u_ref[SUBLANES:SUBLANES + rows, :] = u
    w = cw_ref[...]
    conv = w[CONV_WIDTH - 1:CONV_WIDTH, :] * u
    for k in range(CONV_WIDTH - 1):
        shift = CONV_WIDTH - 1 - k
        conv = conv + w[k:k + 1, :] * u_ref[SUBLANES - shift:SUBLANES - shift + rows, :]
    y = (b_gate * conv).astype(BF16)
    o_ref[...] = x + _dot(y, wout_ref[...])


def _conv_mixer(x3, norm_rows, g_row, w_in, conv_w, w_out, j):
    batch, seq, d = x3.shape
    assert seq % CONV_ROWS == 0
    row_tile = pl.BlockSpec((None, CONV_ROWS, d), lambda b, l: (b, l, 0))
    vmem = (4 * CONV_ROWS * d * 4 + (3 * d * d + d * d) * 2
            + (CONV_ROWS + SUBLANES) * d * 4 + 8 * CONV_ROWS * d * 4)
    return pl.pallas_call(
        _conv_kernel,
        out_shape=jax.ShapeDtypeStruct((batch, seq, d), F32),
        grid=(batch, seq // CONV_ROWS),
        in_specs=[
            row_tile,
            pl.BlockSpec((None, 1, d), lambda b, l: (g_row, 0, 0)),
            _resident((None, d, 3 * d), lambda b, l: (j, 0, 0)),
            pl.BlockSpec((None, CONV_WIDTH, d), lambda b, l: (j, 0, 0)),
            _resident((None, d, d), lambda b, l: (j, 0, 0)),
        ],
        out_specs=row_tile,
        scratch_shapes=[pltpu.VMEM((CONV_ROWS + SUBLANES, d), F32)],
        compiler_params=_params(("arbitrary", "arbitrary"), vmem),
        name="conv_mixer",
    )(x3, norm_rows, w_in, conv_w, w_out)


XA_ROWS = 1024


def _kv_kernel(m_ref, g_ref, wkv_ref, kt_ref, v_ref):
    d = m_ref.shape[1]
    mn = _rms(m_ref[...], g_ref[...]).astype(BF16)
    kv = _dot(mn, wkv_ref[...])
    kt_ref[...] = kv[:, :d].T.astype(BF16)
    v_ref[...] = kv[:, d:].astype(BF16)


def _memory_kv(mem, norm_rows, g_row, w_kv, layer):
    batch, mlen, d = mem.shape
    vmem = 2 * mlen * d * 4 + 2 * d * 2 * d * 2 + 4 * mlen * d * 2 + 8 * mlen * d * 4
    return pl.pallas_call(
        _kv_kernel,
        out_shape=(jax.ShapeDtypeStruct((batch, d, mlen), BF16),
                   jax.ShapeDtypeStruct((batch, mlen, d), BF16)),
        grid=(batch,),
        in_specs=[
            pl.BlockSpec((None, mlen, d), lambda b: (b, 0, 0)),
            pl.BlockSpec((None, 1, d), lambda b: (g_row, 0, 0)),
            _resident((None, d, 2 * d), lambda b: (layer, 0, 0)),
        ],
        out_specs=(pl.BlockSpec((None, d, mlen), lambda b: (b, 0, 0)),
                   pl.BlockSpec((None, mlen, d), lambda b: (b, 0, 0))),
        compiler_params=_params(("parallel",), vmem),
        name="memory_kv",
    )(mem, norm_rows, w_kv)


def _xattn_kernel(x_ref, g_ref, wq_ref, kt_ref, v_ref, wo_ref, o_ref, cat_ref):
    d = x_ref.shape[1]
    hd = d // XA_HEADS
    scale = hd ** -0.5
    x = x_ref[...]
    h = _rms(x, g_ref[...]).astype(BF16)
    q = _dot(h, wq_ref[...]).astype(BF16)
    for a in range(XA_HEADS):
        lo = a * hd
        s = _dot(q[:, lo:lo + hd], kt_ref[lo:lo + hd, :]) * scale
        e = jnp.exp(s - jnp.max(s, axis=-1, keepdims=True))
        p = (e / jnp.sum(e, axis=-1, keepdims=True)).astype(BF16)
        cat_ref[:, lo:lo + hd] = _dot(p, v_ref[:, lo:lo + hd]).astype(BF16)
    o_ref[...] = x + _dot(cat_ref[...], wo_ref[...])


def _cross_attention(x3, norm_rows, g_row, w_q, kt, v, w_o, layer):
    batch, seq, d = x3.shape
    mlen = v.shape[1]
    assert seq % XA_ROWS == 0
    row_tile = pl.BlockSpec((None, XA_ROWS, d), lambda b, l: (b, l, 0))
    vmem = (4 * XA_ROWS * d * 4 + 2 * d * d * 2 + 4 * mlen * d * 2
            + XA_ROWS * d * 2 + 8 * XA_ROWS * d * 4)
    return pl.pallas_call(
        _xattn_kernel,
        out_shape=jax.ShapeDtypeStruct((batch, seq, d), F32),
        grid=(batch, seq // XA_ROWS),
        in_specs=[
            row_tile,
            pl.BlockSpec((None, 1, d), lambda b, l: (g_row, 0, 0)),
            _resident((None, d, d), lambda b, l: (layer, 0, 0)),
            pl.BlockSpec((None, d, mlen), lambda b, l: (b, 0, 0)),
            pl.BlockSpec((None, mlen, d), lambda b, l: (b, 0, 0)),
            _resident((None, d, d), lambda b, l: (layer, 0, 0)),
        ],
        out_specs=row_tile,
        scratch_shapes=[pltpu.VMEM((XA_ROWS, d), BF16)],
        compiler_params=_params(("parallel", "parallel"), vmem),
        name="cross_attention",
    )(x3, norm_rows, w_q, kt, v, w_o)


S5_STEPS = 32


def _s5_disc_kernel(are_ref, aim_ref, ldt_ref, arex_ref, aimx_ref, ldtx_ref,
                    bre_ref, bim_ref, abr_ref, abi_ref, bbr_ref, bbi_ref):
    def abar(a_re, a_im, log_dt):
        lam_re = jnp.minimum(a_re, EIG_CLIP)
        dt = jnp.exp(log_dt)
        mag = jnp.exp(lam_re * dt)
        return lam_re, a_im, mag * jnp.cos(a_im * dt), mag * jnp.sin(a_im * dt)

    _, _, ab_re, ab_im = abar(are_ref[...], aim_ref[...], ldt_ref[...])
    abr_ref[...] = ab_re
    abi_ref[...] = ab_im
    lam_re, lam_im, ab_re, ab_im = abar(arex_ref[...], aimx_ref[...], ldtx_ref[...])
    den = lam_re * lam_re + lam_im * lam_im
    num_re = ab_re - 1.0
    num_im = ab_im
    coef_re = (num_re * lam_re + num_im * lam_im) / den
    coef_im = (num_im * lam_re - num_re * lam_im) / den
    br = bre_ref[...]
    bi = bim_ref[...]
    bbr_ref[...] = coef_re * br - coef_im * bi
    bbi_ref[...] = coef_re * bi + coef_im * br


def _s5_discretise(a_re, a_im, log_dt, b_re, b_im):
    nl, g, p = a_re.shape
    h = b_re.shape[-1]
    ldt = jnp.broadcast_to(log_dt[:, :, None], (nl, g, p))
    rep = lambda a: jnp.repeat(a, h, axis=-1)
    small = pl.BlockSpec((None, g, p), lambda i: (i, 0, 0))
    big = pl.BlockSpec((None, g, p * h), lambda i: (i, 0, 0))
    return pl.pallas_call(
        _s5_disc_kernel,
        out_shape=(jax.ShapeDtypeStruct((nl, g, p), F32),) * 2
        + (jax.ShapeDtypeStruct((nl, g, p * h), F32),) * 2,
        grid=(nl,),
        in_specs=[small, small, small, big, big, big, big, big],
        out_specs=(small, small, big, big),
        compiler_params=_params(("parallel",), 16 * g * p * h * 4 * 4),
        name="s5_discretise",
    )(a_re, a_im, ldt, rep(a_re), rep(a_im), rep(ldt),
      b_re.reshape(nl, g, p * h), b_im.reshape(nl, g, p * h))


def _s5_pack(abar_re, abar_im, bbar_re, bbar_im, c_re, c_im):
    nl, g, p = abar_re.shape
    h = SSM_GROUP
    nblk = g // GROUPS_PER_BLOCK
    eye_q = jnp.eye(PAIRS_PER_BLOCK, dtype=F32)
    eye_g = jnp.eye(2, dtype=F32)
    split = lambda a: a.reshape((nl, nblk, PAIRS_PER_BLOCK, 2) + a.shape[2:])
    b = jnp.stack([split(bbar_re.reshape(nl, g, p, h)),
                   split(bbar_im.reshape(nl, g, p, h))], axis=1)
    wb = jnp.einsum("lcjqgph,qr,gs->ljrshqcgp", b, eye_q, eye_g)
    wb = wb.reshape(nl, nblk, LANES, STATE_COLS).astype(BF16)
    c = jnp.stack([split(c_re), split(-c_im)], axis=1)
    wc = jnp.einsum("lcjqghp,qr,gs->ljqcgprsh", c, eye_q, eye_g)
    wc = wc.reshape(nl, nblk, STATE_COLS, LANES).astype(BF16)
    a = jnp.stack([abar_re, abar_im], axis=1)
    a = a.reshape(nl, 2, nblk, PAIRS_PER_BLOCK, 2 * p)
    a = a.transpose(0, 2, 3, 1, 4).reshape(nl, nblk, 2 * PAIRS_PER_BLOCK, 2 * p)
    return wb, wc, a


def _s5_kernel(x_ref, g_ref, wb_ref, wc_ref, a_ref, dskip_ref, wglu_ref, o_ref,
               hn_ref, bu_ref, st_ref, z_ref, carry_ref):
    batch, steps, d = x_ref.shape
    rows = batch * steps
    nblk = d // LANES

    @pl.when(pl.program_id(0) == 0)
    def _():
        carry_ref[...] = jnp.zeros(carry_ref.shape, F32)

    x = x_ref[...].reshape(rows, d)
    hn = _rms(x, g_ref[...])
    for j in range(nblk):
        hn_ref[j] = hn[:, j * LANES:(j + 1) * LANES]

    for j in range(nblk):
        bu = bu_ref.at[j % 2]
        st = st_ref.at[j % 2]
        blk = hn_ref[j].reshape(batch, steps, LANES)
        hnt = jnp.swapaxes(blk, 0, 1).reshape(rows, LANES).astype(BF16)
        bu[...] = _dot(hnt, wb_ref[j])

        a = a_ref[j]
        coef = [jnp.broadcast_to(a[r:r + 1, :], (batch, LANES))
                for r in range(2 * PAIRS_PER_BLOCK)]
        state = [carry_ref[j, r] for r in range(2 * PAIRS_PER_BLOCK)]
        for t in range(steps):
            r0 = t * batch
            for q in range(PAIRS_PER_BLOCK):
                s_re, s_im = state[2 * q], state[2 * q + 1]
                a_re, a_im = coef[2 * q], coef[2 * q + 1]
                c_re = q * 2 * LANES
                c_im = c_re + LANES
                n_re = a_re * s_re - a_im * s_im + bu[r0:r0 + batch, c_re:c_re + LANES]
                n_im = a_re * s_im + a_im * s_re + bu[r0:r0 + batch, c_im:c_im + LANES]
                st[r0:r0 + batch, c_re:c_re + LANES] = n_re.astype(BF16)
                st[r0:r0 + batch, c_im:c_im + LANES] = n_im.astype(BF16)
                state[2 * q], state[2 * q + 1] = n_re, n_im
        for r in range(2 * PAIRS_PER_BLOCK):
            carry_ref[j, r] = state[r]

        yt = _dot(st[...], wc_ref[j])
        y = jnp.swapaxes(yt.reshape(steps, batch, LANES), 0, 1).reshape(rows, LANES)
        y = y + dskip_ref[:, j * LANES:(j + 1) * LANES] * hn_ref[j]
        z_ref[:, j * LANES:(j + 1) * LANES] = jax.nn.gelu(y).astype(BF16)

    glu = _dot(z_ref[...], wglu_ref[...])
    out = x_ref[...].reshape(rows, d) + glu[:, :d] * jax.nn.sigmoid(glu[:, d:])
    o_ref[...] = out.reshape(batch, steps, d)


def _s5_mixer(x3, norm_rows, g_row, wb, wc, a, d_skip, w_glu, j):
    batch, seq, d = x3.shape
    steps = S5_STEPS
    assert seq % steps == 0 and steps % SUBLANES == 0 and batch % (2 * SUBLANES) == 0
    rows = batch * steps
    nblk = d // LANES
    vmem = (4 * rows * d * 4
            + 2 * 2 * nblk * LANES * STATE_COLS * 2
            + 2 * d * d * 2
            + 2 * rows * d * 4 + rows * STATE_COLS * 6 + rows * d * 2
            + 8 * rows * d * 4)
    return pl.pallas_call(
        _s5_kernel,
        out_shape=jax.ShapeDtypeStruct((batch, seq, d), F32),
        grid=(seq // steps,),
        in_specs=[
            pl.BlockSpec((batch, steps, d), lambda c: (0, c, 0)),
            pl.BlockSpec((None, 1, d), lambda c: (g_row, 0, 0)),
            _resident((None, nblk, LANES, STATE_COLS), lambda c: (j, 0, 0, 0)),
            _resident((None, nblk, STATE_COLS, LANES), lambda c: (j, 0, 0, 0)),
            _resident((None, nblk, 2 * PAIRS_PER_BLOCK, LANES), lambda c: (j, 0, 0, 0)),
            pl.BlockSpec((None, 1, d), lambda c: (j, 0, 0)),
            _resident((None, d, 2 * d), lambda c: (j, 0, 0)),
        ],
        out_specs=pl.BlockSpec((batch, steps, d), lambda c: (0, c, 0)),
        scratch_shapes=[
            pltpu.VMEM((nblk, rows, LANES), F32),
            pltpu.VMEM((2, rows, STATE_COLS), F32),
            pltpu.VMEM((2, rows, STATE_COLS), BF16),
            pltpu.VMEM((rows, d), BF16),
            pltpu.VMEM((nblk, 2 * PAIRS_PER_BLOCK, batch, LANES), F32),
        ],
        compiler_params=_params(("arbitrary",), vmem),
        name="s5_mixer",
    )(x3, norm_rows, wb, wc, a, d_skip, w_glu)


def kernel(x, mem, norm_g, final_g, ffn1_up, ffn1_down, ffn2_up, ffn2_down,
           conv_w_in, conv_w, conv_w_out,
           ssm_a_re, ssm_a_im, ssm_log_dt, ssm_b_re, ssm_b_im,
           ssm_c_re, ssm_c_im, ssm_d, ssm_w_glu,
           xa_w_q, xa_w_kv, xa_w_o):
    batch, seq, d = x.shape
    depth = norm_g.shape[0]
    assert norm_g.shape[1] == N_NORMS
    assert ssm_b_re.shape[-1] == SSM_GROUP and ssm_a_re.shape[-1] == SSM_STATE

    norm_rows = norm_g.reshape(depth * N_NORMS, 1, d)
    final_row = final_g.reshape(1, d)
    bf = lambda w: w.astype(BF16)
    ffn1_up, ffn1_down, ffn2_up, ffn2_down = map(bf, (ffn1_up, ffn1_down, ffn2_up, ffn2_down))
    conv_w_in, conv_w_out, ssm_w_glu = map(bf, (conv_w_in, conv_w_out, ssm_w_glu))
    xa_w_q, xa_w_kv, xa_w_o = map(bf, (xa_w_q, xa_w_kv, xa_w_o))

    abar_re, abar_im, bbar_re, bbar_im = _s5_discretise(
        ssm_a_re, ssm_a_im, ssm_log_dt, ssm_b_re, ssm_b_im)
    s5_wb, s5_wc, s5_a = _s5_pack(abar_re, abar_im, bbar_re, bbar_im, ssm_c_re, ssm_c_im)
    s5_d = ssm_d.reshape(ssm_d.shape[0], 1, d)

    for i in range(depth):
        row = i * N_NORMS
        j = i // 2
        x = _ffn(x, norm_rows, row + 0, ffn1_up, ffn1_down, i, final_row, False)
        if i % 2 == 0:
            x = _conv_mixer(x, norm_rows, row + 1, conv_w_in, conv_w, conv_w_out, j)
        else:
            x = _s5_mixer(x, norm_rows, row + 1, s5_wb, s5_wc, s5_a, s5_d, ssm_w_glu, j)
        kt, v = _memory_kv(mem, norm_rows, row + 3, xa_w_kv, i)
        x = _cross_attention(x, norm_rows, row + 2, xa_w_q, kt, v, xa_w_o, i)
        x = _ffn(x, norm_rows, row + 4, ffn2_up, ffn2_down, i, final_row, i == depth - 1)
    return x
```

```python
import functools
import math

import jax
import jax.numpy as jnp
from jax import lax
from jax.experimental import pallas as pl
from jax.experimental.pallas import tpu as pltpu

F32 = jnp.float32
BF16 = jnp.bfloat16

NORM_EPS = 1e-6
EIG_CLIP = -1e-4
N_NORMS = 5
CONV_WIDTH = 3
SSM_GROUP = 16
SSM_STATE = 64
XA_HEADS = 4

LANES = 128
SUBLANES = 8
V7X_VMEM_BYTES = 64 * 1024 * 1024

GROUPS_PER_BLOCK = LANES // SSM_GROUP
PAIRS_PER_BLOCK = GROUPS_PER_BLOCK // 2
STATE_COLS = GROUPS_PER_BLOCK * 2 * SSM_STATE


def _rms(x, g):
    ms = jnp.mean(x * x, axis=-1, keepdims=True)
    return x * lax.rsqrt(ms + NORM_EPS) * g


def _dot(a, b):
    return jnp.dot(a, b, preferred_element_type=F32)


def _resident(block_shape, index_map):
    return pl.BlockSpec(block_shape, index_map, pipeline_mode=pl.Buffered(1))


def _params(semantics, vmem_bytes):
    return pltpu.CompilerParams(dimension_semantics=semantics,
                                vmem_limit_bytes=min(vmem_bytes, V7X_VMEM_BYTES))


FFN_ROWS = 512
FFN_CHUNK = 256


FFN_LOAD_STEPS = 11


def _ffn_kernel(x_ref, g_ref, wup_ref, wdn_ref, fg_ref, o_ref, wup_bf, wdn_bf, act_ref,
                *, final_norm):
    step = pl.program_id(0)
    n_load, _, up_cols = wup_bf.shape
    d_ff = wdn_bf.shape[0]
    dn_rows = d_ff // n_load

    @pl.when(step < n_load)
    def _():
        wup_bf[step] = wup_ref[...].astype(BF16)
        first = pl.multiple_of(step * dn_rows, dn_rows)
        wdn_bf[pl.ds(first, dn_rows), :] = wdn_ref[...].astype(BF16)

    def up_weight(lo):
        chunk, off = divmod(lo, up_cols)
        assert off + FFN_CHUNK <= up_cols
        return wup_bf[chunk, :, off:off + FFN_CHUNK]

    @pl.when(step >= n_load)
    def _():
        x = x_ref[...]
        h = _rms(x, g_ref[...]).astype(BF16)
        for c in range(d_ff // FFN_CHUNK):
            lo = c * FFN_CHUNK
            gate = _dot(h, up_weight(lo))
            up = _dot(h, up_weight(d_ff + lo))
            act_ref[:, lo:lo + FFN_CHUNK] = (jax.nn.silu(gate) * up).astype(BF16)
        y = x + 0.5 * _dot(act_ref[...], wdn_bf[...])
        if final_norm:
            y = _rms(y, fg_ref[...])
        o_ref[...] = y


def _ffn(x3, norm_rows, g_row, w_up, w_dn, layer, final_g, final_norm):
    batch, seq, d = x3.shape
    d_ff = w_dn.shape[1]
    n_load = FFN_LOAD_STEPS
    assert seq % FFN_ROWS == 0 and d_ff % FFN_CHUNK == 0
    up_cols = 2 * d_ff // n_load
    dn_rows = d_ff // n_load
    assert up_cols * n_load == 2 * d_ff and up_cols % FFN_CHUNK == 0 and dn_rows % SUBLANES == 0
    tiles_per_seq = seq // FFN_ROWS

    def row_index(i):
        t = jnp.maximum(i - n_load, 0)
        return t // tiles_per_seq, t % tiles_per_seq, 0

    def load_index(i):
        return jnp.minimum(i, n_load - 1)

    row_tile = pl.BlockSpec((None, FFN_ROWS, d), row_index)
    vmem = (4 * FFN_ROWS * d * 4
            + (d * 2 * d_ff + d_ff * d) * 2
            + 2 * (d * up_cols + dn_rows * d) * 4
            + FFN_ROWS * d_ff * 2
            + 8 * FFN_ROWS * d * 4)
    return pl.pallas_call(
        functools.partial(_ffn_kernel, final_norm=final_norm),
        out_shape=jax.ShapeDtypeStruct((batch, seq, d), F32),
        grid=(n_load + batch * tiles_per_seq,),
        in_specs=[
            row_tile,
            pl.BlockSpec((None, 1, d), lambda i: (g_row, 0, 0)),
            pl.BlockSpec((None, d, up_cols), lambda i: (layer, 0, load_index(i))),
            pl.BlockSpec((None, dn_rows, d), lambda i: (layer, load_index(i), 0)),
            pl.BlockSpec((1, d), lambda i: (0, 0)),
        ],
        out_specs=row_tile,
        scratch_shapes=[pltpu.VMEM((n_load, d, up_cols), BF16),
                        pltpu.VMEM((d_ff, d), BF16),
                        pltpu.VMEM((FFN_ROWS, d_ff), BF16)],
        compiler_params=_params(("arbitrary",), vmem),
        name="ffn_final" if final_norm else "ffn",
    )(x3, norm_rows, w_up, w_dn, final_g)


CONV_ROWS = 1024
CONV_SUBTILES = 2


def _conv_kernel(x_ref, g_ref, win_ref, cw_ref, wout_ref, o_ref, u_ref):
    rows, d = x_ref.shape
    sub = rows // CONV_SUBTILES

    @pl.when(pl.program_id(1) == 0)
    def _():
        u_ref[0:SUBLANES, :] = jnp.zeros((SUBLANES, d), F32)

    @pl.when(pl.program_id(1) > 0)
    def _():
        u_ref[0:SUBLANES, :] = u_ref[rows:rows + SUBLANES, :]

    w = cw_ref[...]
    for r in range(CONV_SUBTILES):
        lo = r * sub
        x = x_ref[lo:lo + sub, :]
        h = _rms(x, g_ref[...]).astype(BF16)
        cbv = _dot(h, win_ref[...])
        u = cbv[:, :d] * cbv[:, 2 * d:]
        b_gate = cbv[:, d:2 * d]
        u_ref[SUBLANES + lo:SUBLANES + lo + sub, :] = u
        conv = w[CONV_WIDTH - 1:CONV_WIDTH, :] * u
        for k in range(CONV_WIDTH - 1):
            first = SUBLANES + lo - (CONV_WIDTH - 1 - k)
            conv = conv + w[k:k + 1, :] * u_ref[first:first + sub, :]
        y = (b_gate * conv).astype(BF16)
        o_ref[lo:lo + sub, :] = x + _dot(y, wout_ref[...])


def _conv_mixer(x3, norm_rows, g_row, w_in, conv_w, w_out, j):
    batch, seq, d = x3.shape
    assert seq % CONV_ROWS == 0
    row_tile = pl.BlockSpec((None, CONV_ROWS, d), lambda b, l: (b, l, 0))
    vmem = (4 * CONV_ROWS * d * 4 + (3 * d * d + d * d) * 2
            + (CONV_ROWS + SUBLANES) * d * 4 + 8 * CONV_ROWS * d * 4)
    return pl.pallas_call(
        _conv_kernel,
        out_shape=jax.ShapeDtypeStruct((batch, seq, d), F32),
        grid=(batch, seq // CONV_ROWS),
        in_specs=[
            row_tile,
            pl.BlockSpec((None, 1, d), lambda b, l: (g_row, 0, 0)),
            _resident((None, d, 3 * d), lambda b, l: (j, 0, 0)),
            pl.BlockSpec((None, CONV_WIDTH, d), lambda b, l: (j, 0, 0)),
            _resident((None, d, d), lambda b, l: (j, 0, 0)),
        ],
        out_specs=row_tile,
        scratch_shapes=[pltpu.VMEM((CONV_ROWS + SUBLANES, d), F32)],
        compiler_params=_params(("arbitrary", "arbitrary"), vmem),
        name="conv_mixer",
    )(x3, norm_rows, w_in, conv_w, w_out)


XA_ROWS = 2048
XA_SUBTILES = 2


def _kv_kernel(m_ref, g_ref, wkv_ref, kt_ref, v_ref):
    d = m_ref.shape[1]
    mn = _rms(m_ref[...], g_ref[...]).astype(BF16)
    kv = _dot(mn, wkv_ref[...])
    kt_ref[...] = kv[:, :d].T.astype(BF16)
    v_ref[...] = kv[:, d:].astype(BF16)


def _memory_kv(mem, norm_rows, g_row, w_kv, layer):
    batch, mlen, d = mem.shape
    vmem = 2 * mlen * d * 4 + 2 * d * 2 * d * 2 + 4 * mlen * d * 2 + 8 * mlen * d * 4
    return pl.pallas_call(
        _kv_kernel,
        out_shape=(jax.ShapeDtypeStruct((batch, d, mlen), BF16),
                   jax.ShapeDtypeStruct((batch, mlen, d), BF16)),
        grid=(batch,),
        in_specs=[
            pl.BlockSpec((None, mlen, d), lambda b: (b, 0, 0)),
            pl.BlockSpec((None, 1, d), lambda b: (g_row, 0, 0)),
            _resident((None, d, 2 * d), lambda b: (layer, 0, 0)),
        ],
        out_specs=(pl.BlockSpec((None, d, mlen), lambda b: (b, 0, 0)),
                   pl.BlockSpec((None, mlen, d), lambda b: (b, 0, 0))),
        compiler_params=_params(("parallel",), vmem),
        name="memory_kv",
    )(mem, norm_rows, w_kv)


def _xattn_kernel(x_ref, g_ref, wq_ref, kt_ref, v_ref, wo_ref, o_ref, cat_ref):
    d = x_ref.shape[1]
    hd = d // XA_HEADS
    scale = hd ** -0.5
    sub = x_ref.shape[0] // XA_SUBTILES
    for r in range(XA_SUBTILES):
        rs = slice(r * sub, (r + 1) * sub)
        x = x_ref[rs, :]
        h = _rms(x, g_ref[...]).astype(BF16)
        q = _dot(h, wq_ref[...]).astype(BF16)
        for a in range(XA_HEADS):
            lo = a * hd
            s = _dot(q[:, lo:lo + hd], kt_ref[lo:lo + hd, :]) * scale
            e = jnp.exp(s - jnp.max(s, axis=-1, keepdims=True))
            p = (e / jnp.sum(e, axis=-1, keepdims=True)).astype(BF16)
            cat_ref[rs, lo:lo + hd] = _dot(p, v_ref[:, lo:lo + hd]).astype(BF16)
        o_ref[rs, :] = x + _dot(cat_ref[rs, :], wo_ref[...])


def _cross_attention(x3, norm_rows, g_row, w_q, kt, v, w_o, layer):
    batch, seq, d = x3.shape
    mlen = v.shape[1]
    assert seq % XA_ROWS == 0
    row_tile = pl.BlockSpec((None, XA_ROWS, d), lambda b, l: (b, l, 0))
    vmem = (4 * XA_ROWS * d * 4 + 2 * d * d * 2 + 4 * mlen * d * 2
            + XA_ROWS * d * 2 + 8 * XA_ROWS * d * 4)
    return pl.pallas_call(
        _xattn_kernel,
        out_shape=jax.ShapeDtypeStruct((batch, seq, d), F32),
        grid=(batch, seq // XA_ROWS),
        in_specs=[
            row_tile,
            pl.BlockSpec((None, 1, d), lambda b, l: (g_row, 0, 0)),
            _resident((None, d, d), lambda b, l: (layer, 0, 0)),
            pl.BlockSpec((None, d, mlen), lambda b, l: (b, 0, 0)),
            pl.BlockSpec((None, mlen, d), lambda b, l: (b, 0, 0)),
            _resident((None, d, d), lambda b, l: (layer, 0, 0)),
        ],
        out_specs=row_tile,
        scratch_shapes=[pltpu.VMEM((XA_ROWS, d), BF16)],
        compiler_params=_params(("parallel", "parallel"), vmem),
        name="cross_attention",
    )(x3, norm_rows, w_q, kt, v, w_o)


S5_STEPS = 32


def _s5_disc_kernel(are_ref, aim_ref, ldt_ref, b_ref, abr_ref, abi_ref, wb_ref):
    lam_re = jnp.minimum(are_ref[...], EIG_CLIP)
    lam_im = aim_ref[...]
    dt = jnp.exp(ldt_ref[...])
    mag = jnp.exp(lam_re * dt)
    ab_re = mag * jnp.cos(lam_im * dt)
    ab_im = mag * jnp.sin(lam_im * dt)
    abr_ref[...] = ab_re
    abi_ref[...] = ab_im
    den = lam_re * lam_re + lam_im * lam_im
    num_re = ab_re - 1.0
    num_im = ab_im
    coef_re = (num_re * lam_re + num_im * lam_im) / den
    coef_im = (num_im * lam_re - num_re * lam_im) / den
    for j in range(wb_ref.shape[0]):
        for q in range(PAIRS_PER_BLOCK):
            r = j * PAIRS_PER_BLOCK + q
            cr = coef_re[r:r + 1, :]
            ci = coef_im[r:r + 1, :]
            br = b_ref[0, j, :, q * LANES:(q + 1) * LANES]
            bi = b_ref[1, j, :, q * LANES:(q + 1) * LANES]
            lo = q * 2 * LANES
            wb_ref[j, :, lo:lo + LANES] = (cr * br - ci * bi).astype(BF16)
            wb_ref[j, :, lo + LANES:lo + 2 * LANES] = (cr * bi + ci * br).astype(BF16)


def _s5_prepare(a_re, a_im, log_dt, b_re, b_im, c_re, c_im):
    nl, g, p = a_re.shape
    nblk = g // GROUPS_PER_BLOCK
    npair = nblk * PAIRS_PER_BLOCK
    eye_q = jnp.eye(PAIRS_PER_BLOCK, dtype=F32)
    eye_g = jnp.eye(2, dtype=F32)
    split = lambda a: a.reshape(a.shape[:-3] + (nblk, PAIRS_PER_BLOCK, 2) + a.shape[-2:])
    cols = lambda a: a.reshape(nl, npair, 2 * p)
    ldt = jnp.broadcast_to(log_dt[:, :, None], (nl, g, p))
    b = split(jnp.stack([b_re, b_im], axis=1))
    b = jnp.einsum("lcjqgph,qr,gs->lcjrshqgp", b, eye_q, eye_g)
    b = b.reshape(nl, 2, nblk, LANES, PAIRS_PER_BLOCK * 2 * p)
    c = split(jnp.stack([c_re, -c_im], axis=1))
    wc = jnp.einsum("lcjqghp,qr,gs->ljqcgprsh", c, eye_q, eye_g)
    wc = wc.reshape(nl, nblk, STATE_COLS, LANES).astype(BF16)
    par = pl.BlockSpec((None, npair, 2 * p), lambda i: (i, 0, 0))
    abar_re, abar_im, wb = pl.pallas_call(
        _s5_disc_kernel,
        out_shape=(jax.ShapeDtypeStruct((nl, npair, 2 * p), F32),) * 2
        + (jax.ShapeDtypeStruct((nl, nblk, LANES, STATE_COLS), BF16),),
        grid=(nl,),
        in_specs=[par, par, par,
                  pl.BlockSpec((None,) + b.shape[1:], lambda i: (i, 0, 0, 0, 0))],
        out_specs=(par, par,
                   pl.BlockSpec((None, nblk, LANES, STATE_COLS), lambda i: (i, 0, 0, 0))),
        compiler_params=_params(("parallel",), 16 * b[0].size * 4),
        name="s5_discretise",
    )(cols(a_re), cols(a_im), cols(ldt), b)
    return wb, wc, abar_re, abar_im


def _s5_kernel(x_ref, g_ref, wb_ref, wc_ref, abr_ref, abi_ref, dskip_ref, wglu_ref, o_ref,
               hn_ref, bu_ref, st_ref, z_ref, carry_ref):
    batch, steps, d = x_ref.shape
    rows = batch * steps
    nblk = d // LANES

    @pl.when(pl.program_id(0) == 0)
    def _():
        carry_ref[...] = jnp.zeros(carry_ref.shape, F32)

    x = x_ref[...].reshape(rows, d)
    hn = _rms(x, g_ref[...])
    for j in range(nblk):
        hn_ref[j] = hn[:, j * LANES:(j + 1) * LANES]

    for j in range(nblk):
        bu = bu_ref.at[j % 2]
        st = st_ref.at[j % 2]
        blk = hn_ref[j].reshape(batch, steps, LANES)
        hnt = jnp.swapaxes(blk, 0, 1).reshape(rows, LANES).astype(BF16)
        bu[...] = _dot(hnt, wb_ref[j])

        coef = []
        for q in range(PAIRS_PER_BLOCK):
            r = j * PAIRS_PER_BLOCK + q
            coef += [jnp.broadcast_to(abr_ref[r:r + 1, :], (batch, LANES)),
                     jnp.broadcast_to(abi_ref[r:r + 1, :], (batch, LANES))]
        state = [carry_ref[j, r] for r in range(2 * PAIRS_PER_BLOCK)]
        for t in range(steps):
            r0 = t * batch
            for q in range(PAIRS_PER_BLOCK):
                s_re, s_im = state[2 * q], state[2 * q + 1]
                a_re, a_im = coef[2 * q], coef[2 * q + 1]
                c_re = q * 2 * LANES
                c_im = c_re + LANES
                n_re = a_re * s_re - a_im * s_im + bu[r0:r0 + batch, c_re:c_re + LANES]
                n_im = a_re * s_im + a_im * s_re + bu[r0:r0 + batch, c_im:c_im + LANES]
                st[r0:r0 + batch, c_re:c_re + LANES] = n_re.astype(BF16)
                st[r0:r0 + batch, c_im:c_im + LANES] = n_im.astype(BF16)
                state[2 * q], state[2 * q + 1] = n_re, n_im
        for r in range(2 * PAIRS_PER_BLOCK):
            carry_ref[j, r] = state[r]

        yt = _dot(st[...], wc_ref[j])
        y = jnp.swapaxes(yt.reshape(steps, batch, LANES), 0, 1).reshape(rows, LANES)
        y = y + dskip_ref[:, j * LANES:(j + 1) * LANES] * hn_ref[j]
        z_ref[:, j * LANES:(j + 1) * LANES] = jax.nn.gelu(y).astype(BF16)

    glu = _dot(z_ref[...], wglu_ref[...])
    out = x_ref[...].reshape(rows, d) + glu[:, :d] * jax.nn.sigmoid(glu[:, d:])
    o_ref[...] = out.reshape(batch, steps, d)


def _s5_mixer(x3, norm_rows, g_row, wb, wc, abar_re, abar_im, d_skip, w_glu, j):
    batch, seq, d = x3.shape
    steps = S5_STEPS
    assert seq % steps == 0 and steps % SUBLANES == 0 and batch % (2 * SUBLANES) == 0
    rows = batch * steps
    nblk = d // LANES
    vmem = (4 * rows * d * 4
            + 2 * 2 * nblk * LANES * STATE_COLS * 2
            + 2 * d * d * 2
            + 2 * rows * d * 4 + rows * STATE_COLS * 6 + rows * d * 2
            + 8 * rows * d * 4)
    return pl.pallas_call(
        _s5_kernel,
        out_shape=jax.ShapeDtypeStruct((batch, seq, d), F32),
        grid=(seq // steps,),
        in_specs=[
            pl.BlockSpec((batch, steps, d), lambda c: (0, c, 0)),
            pl.BlockSpec((None, 1, d), lambda c: (g_row, 0, 0)),
            _resident((None, nblk, LANES, STATE_COLS), lambda c: (j, 0, 0, 0)),
            _resident((None, nblk, STATE_COLS, LANES), lambda c: (j, 0, 0, 0)),
            _resident((None, nblk * PAIRS_PER_BLOCK, LANES), lambda c: (j, 0, 0)),
            _resident((None, nblk * PAIRS_PER_BLOCK, LANES), lambda c: (j, 0, 0)),
            pl.BlockSpec((None, 1, d), lambda c: (j, 0, 0)),
            _resident((None, d, 2 * d), lambda c: (j, 0, 0)),
        ],
        out_specs=pl.BlockSpec((batch, steps, d), lambda c: (0, c, 0)),
        scratch_shapes=[
            pltpu.VMEM((nblk, rows, LANES), F32),
            pltpu.VMEM((2, rows, STATE_COLS), F32),
            pltpu.VMEM((2, rows, STATE_COLS), BF16),
            pltpu.VMEM((rows, d), BF16),
            pltpu.VMEM((nblk, 2 * PAIRS_PER_BLOCK, batch, LANES), F32),
        ],
        compiler_params=_params(("arbitrary",), vmem),
        name="s5_mixer",
    )(x3, norm_rows, wb, wc, abar_re, abar_im, d_skip, w_glu)


def kernel(x, mem, norm_g, final_g, ffn1_up, ffn1_down, ffn2_up, ffn2_down,
           conv_w_in, conv_w, conv_w_out,
           ssm_a_re, ssm_a_im, ssm_log_dt, ssm_b_re, ssm_b_im,
           ssm_c_re, ssm_c_im, ssm_d, ssm_w_glu,
           xa_w_q, xa_w_kv, xa_w_o):
    batch, seq, d = x.shape
    depth = norm_g.shape[0]
    assert norm_g.shape[1] == N_NORMS
    assert ssm_b_re.shape[-1] == SSM_GROUP and ssm_a_re.shape[-1] == SSM_STATE

    norm_rows = norm_g.reshape(depth * N_NORMS, 1, d)
    final_row = final_g.reshape(1, d)
    bf = lambda w: w.astype(BF16)
    conv_w_in, conv_w_out, ssm_w_glu = map(bf, (conv_w_in, conv_w_out, ssm_w_glu))
    xa_w_q, xa_w_kv, xa_w_o = map(bf, (xa_w_q, xa_w_kv, xa_w_o))

    s5 = _s5_prepare(ssm_a_re, ssm_a_im, ssm_log_dt, ssm_b_re, ssm_b_im, ssm_c_re, ssm_c_im)
    s5_d = ssm_d.reshape(ssm_d.shape[0], 1, d)

    for i in range(depth):
        row = i * N_NORMS
        j = i // 2
        x = _ffn(x, norm_rows, row + 0, ffn1_up, ffn1_down, i, final_row, False)
        if i % 2 == 0:
            x = _conv_mixer(x, norm_rows, row + 1, conv_w_in, conv_w, conv_w_out, j)
        else:
            x = _s5_mixer(x, norm_rows, row + 1, *s5, s5_d, ssm_w_glu, j)
        kt, v = _memory_kv(mem, norm_rows, row + 3, xa_w_kv, i)
        x = _cross_attention(x, norm_rows, row + 2, xa_w_q, kt, v, xa_w_o, i)
        x = _ffn(x, norm_rows, row + 4, ffn2_up, ffn2_down, i, final_row, i == depth - 1)
    return x
```

```python
import functools
import math

import jax
import jax.numpy as jnp
import numpy as np
from jax import lax
from jax.experimental import pallas as pl
from jax.experimental.pallas import tpu as pltpu

F32 = jnp.float32
BF16 = jnp.bfloat16

NORM_EPS = 1e-6
EIG_CLIP = -1e-4
N_NORMS = 5
CONV_WIDTH = 3
SSM_GROUP = 16
SSM_STATE = 64
XA_HEADS = 4

LANES = 128
SUBLANES = 8
V7X_VMEM_BYTES = 64 * 1024 * 1024

GROUPS_PER_BLOCK = LANES // SSM_GROUP
PAIRS_PER_BLOCK = GROUPS_PER_BLOCK // 2
STATE_COLS = GROUPS_PER_BLOCK * 2 * SSM_STATE


def _rms(x, g):
    ms = jnp.mean(x * x, axis=-1, keepdims=True)
    return x * lax.rsqrt(ms + NORM_EPS) * g


def _dot(a, b):
    return jnp.dot(a, b, preferred_element_type=F32)


def _resident(block_shape, index_map):
    return pl.BlockSpec(block_shape, index_map, pipeline_mode=pl.Buffered(1))


def _params(semantics, vmem_bytes):
    return pltpu.CompilerParams(dimension_semantics=semantics,
                                vmem_limit_bytes=min(vmem_bytes, V7X_VMEM_BYTES))


FFN_ROWS = 1024
FFN_CHUNK = 256
FFN_LOAD_STEPS = 8


def _ffn_kernel(x_ref, g_ref, wup_ref, wdn_ref, fg_ref, o_ref, wup_bf, wdn_bf, act_ref,
                *, final_norm):
    step = pl.program_id(0)
    up_rows = wup_ref.shape[0]
    dn_rows = wdn_ref.shape[0]
    d_ff = wdn_bf.shape[0]

    @pl.when(step < FFN_LOAD_STEPS)
    def _():
        first = pl.multiple_of(step * up_rows, up_rows)
        wup_bf[pl.ds(first, up_rows), :] = wup_ref[...].astype(BF16)
        first = pl.multiple_of(step * dn_rows, dn_rows)
        wdn_bf[pl.ds(first, dn_rows), :] = wdn_ref[...].astype(BF16)

    @pl.when(step >= FFN_LOAD_STEPS)
    def _():
        x = x_ref[...]
        h = _rms(x, g_ref[...]).astype(BF16)
        for c in range(d_ff // FFN_CHUNK):
            lo = c * FFN_CHUNK
            gate = _dot(h, wup_bf[:, lo:lo + FFN_CHUNK])
            up = _dot(h, wup_bf[:, d_ff + lo:d_ff + lo + FFN_CHUNK])
            act_ref[:, lo:lo + FFN_CHUNK] = (jax.nn.silu(gate) * up).astype(BF16)
        y = x + 0.5 * _dot(act_ref[...], wdn_bf[...])
        if final_norm:
            y = _rms(y, fg_ref[...])
        o_ref[...] = y


def _ffn(x3, norm_rows, g_row, w_up, w_dn, layer, final_g, final_norm):
    batch, seq, d = x3.shape
    d_ff = w_dn.shape[1]
    n_load = FFN_LOAD_STEPS
    bf16_rows = 2 * SUBLANES
    assert seq % FFN_ROWS == 0 and d_ff % FFN_CHUNK == 0
    assert d % (n_load * bf16_rows) == 0 and d_ff % (n_load * bf16_rows) == 0
    up_rows = d // n_load
    dn_rows = d_ff // n_load
    tiles_per_seq = seq // FFN_ROWS

    def row_index(i):
        t = jnp.maximum(i - n_load, 0)
        return t // tiles_per_seq, t % tiles_per_seq, 0

    def load_index(i):
        return jnp.minimum(i, n_load - 1)

    row_tile = pl.BlockSpec((None, FFN_ROWS, d), row_index)
    vmem = (4 * FFN_ROWS * d * 4
            + (d * 2 * d_ff + d_ff * d) * 2
            + 2 * (up_rows * 2 * d_ff + dn_rows * d) * 4
            + FFN_ROWS * d_ff * 2
            + 4 * FFN_ROWS * d * 4)
    return pl.pallas_call(
        functools.partial(_ffn_kernel, final_norm=final_norm),
        out_shape=jax.ShapeDtypeStruct((batch, seq, d), F32),
        grid=(n_load + batch * tiles_per_seq,),
        in_specs=[
            row_tile,
            pl.BlockSpec((None, 1, d), lambda i: (g_row, 0, 0)),
            pl.BlockSpec((None, up_rows, 2 * d_ff), lambda i: (layer, load_index(i), 0)),
            pl.BlockSpec((None, dn_rows, d), lambda i: (layer, load_index(i), 0)),
            pl.BlockSpec((1, d), lambda i: (0, 0)),
        ],
        out_specs=row_tile,
        scratch_shapes=[pltpu.VMEM((d, 2 * d_ff), BF16),
                        pltpu.VMEM((d_ff, d), BF16),
                        pltpu.VMEM((FFN_ROWS, d_ff), BF16)],
        compiler_params=_params(("arbitrary",), vmem),
        name="ffn_final" if final_norm else "ffn",
    )(x3, norm_rows, w_up, w_dn, final_g)


CONV_ROWS = 1024
CONV_SUBTILES = 2


def _conv_kernel(x_ref, g_ref, win_ref, cw_ref, wout_ref, o_ref, u_ref):
    rows, d = x_ref.shape
    sub = rows // CONV_SUBTILES

    @pl.when(pl.program_id(1) == 0)
    def _():
        u_ref[0:SUBLANES, :] = jnp.zeros((SUBLANES, d), F32)

    @pl.when(pl.program_id(1) > 0)
    def _():
        u_ref[0:SUBLANES, :] = u_ref[rows:rows + SUBLANES, :]

    w = cw_ref[...]
    for r in range(CONV_SUBTILES):
        lo = r * sub
        x = x_ref[lo:lo + sub, :]
        h = _rms(x, g_ref[...]).astype(BF16)
        cbv = _dot(h, win_ref[...])
        u = cbv[:, :d] * cbv[:, 2 * d:]
        b_gate = cbv[:, d:2 * d]
        u_ref[SUBLANES + lo:SUBLANES + lo + sub, :] = u
        conv = w[CONV_WIDTH - 1:CONV_WIDTH, :] * u
        for k in range(CONV_WIDTH - 1):
            first = SUBLANES + lo - (CONV_WIDTH - 1 - k)
            conv = conv + w[k:k + 1, :] * u_ref[first:first + sub, :]
        y = (b_gate * conv).astype(BF16)
        o_ref[lo:lo + sub, :] = x + _dot(y, wout_ref[...])


def _conv_mixer(x3, norm_rows, g_row, w_in, conv_w, w_out, j):
    batch, seq, d = x3.shape
    assert seq % CONV_ROWS == 0
    row_tile = pl.BlockSpec((None, CONV_ROWS, d), lambda b, l: (b, l, 0))
    vmem = (4 * CONV_ROWS * d * 4 + (3 * d * d + d * d) * 2
            + (CONV_ROWS + SUBLANES) * d * 4 + 8 * CONV_ROWS * d * 4)
    return pl.pallas_call(
        _conv_kernel,
        out_shape=jax.ShapeDtypeStruct((batch, seq, d), F32),
        grid=(batch, seq // CONV_ROWS),
        in_specs=[
            row_tile,
            pl.BlockSpec((None, 1, d), lambda b, l: (g_row, 0, 0)),
            _resident((None, d, 3 * d), lambda b, l: (j, 0, 0)),
            pl.BlockSpec((None, CONV_WIDTH, d), lambda b, l: (j, 0, 0)),
            _resident((None, d, d), lambda b, l: (j, 0, 0)),
        ],
        out_specs=row_tile,
        scratch_shapes=[pltpu.VMEM((CONV_ROWS + SUBLANES, d), F32)],
        compiler_params=_params(("arbitrary", "arbitrary"), vmem),
        name="conv_mixer",
    )(x3, norm_rows, w_in, conv_w, w_out)


XA_ROWS = 2048
XA_SUBTILES = 2


def _kv_kernel(m_ref, g_ref, wkv_ref, kt_ref, v_ref):
    d = m_ref.shape[1]
    mn = _rms(m_ref[...], g_ref[...]).astype(BF16)
    kv = _dot(mn, wkv_ref[...])
    kt_ref[...] = kv[:, :d].T.astype(BF16)
    v_ref[...] = kv[:, d:].astype(BF16)


def _memory_kv(mem, norm_rows, g_row, w_kv, layer):
    batch, mlen, d = mem.shape
    vmem = 2 * mlen * d * 4 + 2 * d * 2 * d * 2 + 4 * mlen * d * 2 + 8 * mlen * d * 4
    return pl.pallas_call(
        _kv_kernel,
        out_shape=(jax.ShapeDtypeStruct((batch, d, mlen), BF16),
                   jax.ShapeDtypeStruct((batch, mlen, d), BF16)),
        grid=(batch,),
        in_specs=[
            pl.BlockSpec((None, mlen, d), lambda b: (b, 0, 0)),
            pl.BlockSpec((None, 1, d), lambda b: (g_row, 0, 0)),
            _resident((None, d, 2 * d), lambda b: (layer, 0, 0)),
        ],
        out_specs=(pl.BlockSpec((None, d, mlen), lambda b: (b, 0, 0)),
                   pl.BlockSpec((None, mlen, d), lambda b: (b, 0, 0))),
        compiler_params=_params(("parallel",), vmem),
        name="memory_kv",
    )(mem, norm_rows, w_kv)


def _xattn_kernel(x_ref, g_ref, wq_ref, kt_ref, v_ref, wo_ref, o_ref, cat_ref):
    d = x_ref.shape[1]
    hd = d // XA_HEADS
    scale = hd ** -0.5
    sub = x_ref.shape[0] // XA_SUBTILES
    for r in range(XA_SUBTILES):
        rs = slice(r * sub, (r + 1) * sub)
        x = x_ref[rs, :]
        h = _rms(x, g_ref[...]).astype(BF16)
        q = _dot(h, wq_ref[...]).astype(BF16)
        for a in range(XA_HEADS):
            lo = a * hd
            s = _dot(q[:, lo:lo + hd], kt_ref[lo:lo + hd, :]) * scale
            e = jnp.exp(s - jnp.max(s, axis=-1, keepdims=True))
            p = (e / jnp.sum(e, axis=-1, keepdims=True)).astype(BF16)
            cat_ref[rs, lo:lo + hd] = _dot(p, v_ref[:, lo:lo + hd]).astype(BF16)
        o_ref[rs, :] = x + _dot(cat_ref[rs, :], wo_ref[...])


def _cross_attention(x3, norm_rows, g_row, w_q, kt, v, w_o, layer):
    batch, seq, d = x3.shape
    mlen = v.shape[1]
    assert seq % XA_ROWS == 0
    row_tile = pl.BlockSpec((None, XA_ROWS, d), lambda b, l: (b, l, 0))
    vmem = (4 * XA_ROWS * d * 4 + 2 * d * d * 2 + 4 * mlen * d * 2
            + XA_ROWS * d * 2 + 8 * XA_ROWS * d * 4)
    return pl.pallas_call(
        _xattn_kernel,
        out_shape=jax.ShapeDtypeStruct((batch, seq, d), F32),
        grid=(batch, seq // XA_ROWS),
        in_specs=[
            row_tile,
            pl.BlockSpec((None, 1, d), lambda b, l: (g_row, 0, 0)),
            _resident((None, d, d), lambda b, l: (layer, 0, 0)),
            pl.BlockSpec((None, d, mlen), lambda b, l: (b, 0, 0)),
            pl.BlockSpec((None, mlen, d), lambda b, l: (b, 0, 0)),
            _resident((None, d, d), lambda b, l: (layer, 0, 0)),
        ],
        out_specs=row_tile,
        scratch_shapes=[pltpu.VMEM((XA_ROWS, d), BF16)],
        compiler_params=_params(("parallel", "parallel"), vmem),
        name="cross_attention",
    )(x3, norm_rows, w_q, kt, v, w_o)


S5_STEPS = 32


def _s5_disc_kernel(are_ref, aim_ref, ldt_ref, b_ref, abr_ref, abi_ref, wb_ref):
    lam_re = jnp.minimum(are_ref[...], EIG_CLIP)
    lam_im = aim_ref[...]
    dt = jnp.exp(ldt_ref[...])
    mag = jnp.exp(lam_re * dt)
    ab_re = mag * jnp.cos(lam_im * dt)
    ab_im = mag * jnp.sin(lam_im * dt)
    abr_ref[...] = ab_re
    abi_ref[...] = ab_im
    den = lam_re * lam_re + lam_im * lam_im
    num_re = ab_re - 1.0
    num_im = ab_im
    coef_re = (num_re * lam_re + num_im * lam_im) / den
    coef_im = (num_im * lam_re - num_re * lam_im) / den
    for j in range(wb_ref.shape[0]):
        for q in range(PAIRS_PER_BLOCK):
            r = j * PAIRS_PER_BLOCK + q
            cr = coef_re[r:r + 1, :]
            ci = coef_im[r:r + 1, :]
            br = b_ref[0, j, :, q * LANES:(q + 1) * LANES]
            bi = b_ref[1, j, :, q * LANES:(q + 1) * LANES]
            lo = q * 2 * LANES
            wb_ref[j, :, lo:lo + LANES] = (cr * br - ci * bi).astype(BF16)
            wb_ref[j, :, lo + LANES:lo + 2 * LANES] = (cr * bi + ci * br).astype(BF16)


def _s5_prepare(a_re, a_im, log_dt, b_re, b_im, c_re, c_im):
    nl, g, p = a_re.shape
    h = b_re.shape[-1]
    nblk = g // GROUPS_PER_BLOCK
    npair = nblk * PAIRS_PER_BLOCK
    cols = lambda a: a.reshape(nl, npair, 2 * p)
    ldt = jnp.broadcast_to(log_dt[:, :, None], (nl, g, p))
    same_group = (np.arange(LANES)[:, None] // h) == (np.arange(GROUPS_PER_BLOCK * p)[None, :] // p)
    b = jnp.stack([b_re, b_im], axis=1).reshape(nl, 2, nblk, GROUPS_PER_BLOCK, p, h)
    b = b.transpose(0, 1, 2, 5, 3, 4).reshape(nl, 2, nblk, h, GROUPS_PER_BLOCK * p)
    b = jnp.where(same_group, jnp.tile(b, (1, 1, 1, GROUPS_PER_BLOCK, 1)), 0.0)
    row = np.arange(STATE_COLS)
    row_group = 2 * (row // (4 * p)) + (row // p) % 2
    same_group = row_group[:, None] == (np.arange(LANES)[None, :] // h)
    c = jnp.stack([c_re, -c_im], axis=1).reshape(nl, 2, nblk, PAIRS_PER_BLOCK, 2, h, p)
    c = c.transpose(0, 2, 3, 1, 4, 6, 5).reshape(nl, nblk, STATE_COLS, h)
    wc = jnp.where(same_group, jnp.tile(c, (1, 1, 1, GROUPS_PER_BLOCK)), 0.0).astype(BF16)
    par = pl.BlockSpec((None, npair, 2 * p), lambda i: (i, 0, 0))
    abar_re, abar_im, wb = pl.pallas_call(
        _s5_disc_kernel,
        out_shape=(jax.ShapeDtypeStruct((nl, npair, 2 * p), F32),) * 2
        + (jax.ShapeDtypeStruct((nl, nblk, LANES, STATE_COLS), BF16),),
        grid=(nl,),
        in_specs=[par, par, par,
                  pl.BlockSpec((None,) + b.shape[1:], lambda i: (i, 0, 0, 0, 0))],
        out_specs=(par, par,
                   pl.BlockSpec((None, nblk, LANES, STATE_COLS), lambda i: (i, 0, 0, 0))),
        compiler_params=_params(("parallel",), 16 * b[0].size * 4),
        name="s5_discretise",
    )(cols(a_re), cols(a_im), cols(ldt), b)
    return wb, wc, abar_re, abar_im


def _s5_kernel(x_ref, g_ref, wb_ref, wc_ref, abr_ref, abi_ref, dskip_ref, wglu_ref, o_ref,
               hn_ref, bu_ref, st_ref, z_ref, carry_ref):
    batch, steps, d = x_ref.shape
    rows = batch * steps
    nblk = d // LANES

    @pl.when(pl.program_id(0) == 0)
    def _():
        carry_ref[...] = jnp.zeros(carry_ref.shape, F32)

    x = x_ref[...].reshape(rows, d)
    hn = _rms(x, g_ref[...])
    for j in range(nblk):
        hn_ref[j] = hn[:, j * LANES:(j + 1) * LANES]

    for j in range(nblk):
        bu = bu_ref.at[j % 2]
        st = st_ref.at[j % 2]
        blk = hn_ref[j].reshape(batch, steps, LANES)
        hnt = jnp.swapaxes(blk, 0, 1).reshape(rows, LANES).astype(BF16)
        bu[...] = _dot(hnt, wb_ref[j])

        coef = []
        for q in range(PAIRS_PER_BLOCK):
            r = j * PAIRS_PER_BLOCK + q
            coef += [jnp.broadcast_to(abr_ref[r:r + 1, :], (batch, LANES)),
                     jnp.broadcast_to(abi_ref[r:r + 1, :], (batch, LANES))]
        state = [carry_ref[j, r] for r in range(2 * PAIRS_PER_BLOCK)]
        for t in range(steps):
            r0 = t * batch
            for q in range(PAIRS_PER_BLOCK):
                s_re, s_im = state[2 * q], state[2 * q + 1]
                a_re, a_im = coef[2 * q], coef[2 * q + 1]
                c_re = q * 2 * LANES
                c_im = c_re + LANES
                n_re = a_re * s_re - a_im * s_im + bu[r0:r0 + batch, c_re:c_re + LANES]
                n_im = a_re * s_im + a_im * s_re + bu[r0:r0 + batch, c_im:c_im + LANES]
                st[r0:r0 + batch, c_re:c_re + LANES] = n_re.astype(BF16)
                st[r0:r0 + batch, c_im:c_im + LANES] = n_im.astype(BF16)
                state[2 * q], state[2 * q + 1] = n_re, n_im
        for r in range(2 * PAIRS_PER_BLOCK):
            carry_ref[j, r] = state[r]

        yt = _dot(st[...], wc_ref[j])
        y = jnp.swapaxes(yt.reshape(steps, batch, LANES), 0, 1).reshape(rows, LANES)
        y = y + dskip_ref[:, j * LANES:(j + 1) * LANES] * hn_ref[j]
        z_ref[:, j * LANES:(j + 1) * LANES] = jax.nn.gelu(y).astype(BF16)

    glu = _dot(z_ref[...], wglu_ref[...])
    out = x_ref[...].reshape(rows, d) + glu[:, :d] * jax.nn.sigmoid(glu[:, d:])
    o_ref[...] = out.reshape(batch, steps, d)


def _s5_mixer(x3, norm_rows, g_row, wb, wc, abar_re, abar_im, d_skip, w_glu, j):
    batch, seq, d = x3.shape
    steps = S5_STEPS
    assert seq % steps == 0 and steps % SUBLANES == 0 and batch % (2 * SUBLANES) == 0
    rows = batch * steps
    nblk = d // LANES
    vmem = (4 * rows * d * 4
            + 2 * 2 * nblk * LANES * STATE_COLS * 2
            + 2 * d * d * 2
            + 2 * rows * d * 4 + rows * STATE_COLS * 6 + rows * d * 2
            + 8 * rows * d * 4)
    return pl.pallas_call(
        _s5_kernel,
        out_shape=jax.ShapeDtypeStruct((batch, seq, d), F32),
        grid=(seq // steps,),
        in_specs=[
            pl.BlockSpec((batch, steps, d), lambda c: (0, c, 0)),
            pl.BlockSpec((None, 1, d), lambda c: (g_row, 0, 0)),
            _resident((None, nblk, LANES, STATE_COLS), lambda c: (j, 0, 0, 0)),
            _resident((None, nblk, STATE_COLS, LANES), lambda c: (j, 0, 0, 0)),
            _resident((None, nblk * PAIRS_PER_BLOCK, LANES), lambda c: (j, 0, 0)),
            _resident((None, nblk * PAIRS_PER_BLOCK, LANES), lambda c: (j, 0, 0)),
            pl.BlockSpec((None, 1, d), lambda c: (j, 0, 0)),
            _resident((None, d, 2 * d), lambda c: (j, 0, 0)),
        ],
        out_specs=pl.BlockSpec((batch, steps, d), lambda c: (0, c, 0)),
        scratch_shapes=[
            pltpu.VMEM((nblk, rows, LANES), F32),
            pltpu.VMEM((2, rows, STATE_COLS), F32),
            pltpu.VMEM((2, rows, STATE_COLS), BF16),
            pltpu.VMEM((rows, d), BF16),
            pltpu.VMEM((nblk, 2 * PAIRS_PER_BLOCK, batch, LANES), F32),
        ],
        compiler_params=_params(("arbitrary",), vmem),
        name="s5_mixer",
    )(x3, norm_rows, wb, wc, abar_re, abar_im, d_skip, w_glu)


def kernel(x, mem, norm_g, final_g, ffn1_up, ffn1_down, ffn2_up, ffn2_down,
           conv_w_in, conv_w, conv_w_out,
           ssm_a_re, ssm_a_im, ssm_log_dt, ssm_b_re, ssm_b_im,
           ssm_c_re, ssm_c_im, ssm_d, ssm_w_glu,
           xa_w_q, xa_w_kv, xa_w_o):
    batch, seq, d = x.shape
    depth = norm_g.shape[0]
    assert norm_g.shape[1] == N_NORMS
    assert ssm_b_re.shape[-1] == SSM_GROUP and ssm_a_re.shape[-1] == SSM_STATE

    norm_rows = norm_g.reshape(depth * N_NORMS, 1, d)
    final_row = final_g.reshape(1, d)
    bf = lambda w: w.astype(BF16)
    conv_w_in, conv_w_out, ssm_w_glu = map(bf, (conv_w_in, conv_w_out, ssm_w_glu))
    xa_w_q, xa_w_kv, xa_w_o = map(bf, (xa_w_q, xa_w_kv, xa_w_o))

    s5 = _s5_prepare(ssm_a_re, ssm_a_im, ssm_log_dt, ssm_b_re, ssm_b_im, ssm_c_re, ssm_c_im)
    s5_d = ssm_d.reshape(ssm_d.shape[0], 1, d)

    for i in range(depth):
        row = i * N_NORMS
        j = i // 2
        x = _ffn(x, norm_rows, row + 0, ffn1_up, ffn1_down, i, final_row, False)
        if i % 2 == 0:
            x = _conv_mixer(x, norm_rows, row + 1, conv_w_in, conv_w, conv_w_out, j)
        else:
            x = _s5_mixer(x, norm_rows, row + 1, *s5, s5_d, ssm_w_glu, j)
        kt, v = _memory_kv(mem, norm_rows, row + 3, xa_w_kv, i)
        x = _cross_attention(x, norm_rows, row + 2, xa_w_q, kt, v, xa_w_o, i)
        x = _ffn(x, norm_rows, row + 4, ffn2_up, ffn2_down, i, final_row, i == depth - 1)
    return x
```

```python
import functools
import math

import jax
import jax.numpy as jnp
import numpy as np
from jax import lax
from jax.experimental import pallas as pl
from jax.experimental.pallas import tpu as pltpu

F32 = jnp.float32
BF16 = jnp.bfloat16

NORM_EPS = 1e-6
EIG_CLIP = -1e-4
N_NORMS = 5
CONV_WIDTH = 3
SSM_GROUP = 16
SSM_STATE = 64
XA_HEADS = 4

LANES = 128
SUBLANES = 8
V7X_VMEM_BYTES = 64 * 1024 * 1024

GROUPS_PER_BLOCK = LANES // SSM_GROUP
PAIRS_PER_BLOCK = GROUPS_PER_BLOCK // 2
STATE_COLS = GROUPS_PER_BLOCK * 2 * SSM_STATE


def _rms(x, g):
    ms = jnp.mean(x * x, axis=-1, keepdims=True)
    return x * lax.rsqrt(ms + NORM_EPS) * g


def _dot(a, b):
    return jnp.dot(a, b, preferred_element_type=F32)


def _resident(block_shape, index_map):
    return pl.BlockSpec(block_shape, index_map, pipeline_mode=pl.Buffered(1))


def _params(semantics, vmem_bytes):
    return pltpu.CompilerParams(dimension_semantics=semantics,
                                vmem_limit_bytes=min(vmem_bytes, V7X_VMEM_BYTES))


FFN_ROWS = 1024
FFN_CHUNK = 256
FFN_LOAD_STEPS = 8


def _ffn_kernel(x_ref, g_ref, wup_ref, wdn_ref, fg_ref, *refs, final_norm, n_cast):
    cast_in = refs[:n_cast]
    o_ref = refs[n_cast]
    cast_out = refs[n_cast + 1:2 * n_cast + 1]
    wup_bf, wdn_bf, act_ref = refs[2 * n_cast + 1:]
    step = pl.program_id(0)
    up_rows = wup_ref.shape[0]
    dn_rows = wdn_ref.shape[0]
    d_ff = wdn_bf.shape[0]

    @pl.when(step < FFN_LOAD_STEPS)
    def _():
        first = pl.multiple_of(step * up_rows, up_rows)
        wup_bf[pl.ds(first, up_rows), :] = wup_ref[...].astype(BF16)
        first = pl.multiple_of(step * dn_rows, dn_rows)
        wdn_bf[pl.ds(first, dn_rows), :] = wdn_ref[...].astype(BF16)

    @pl.when(step >= FFN_LOAD_STEPS)
    def _():
        for src, dst in zip(cast_in, cast_out):
            dst[...] = src[...].astype(BF16)
        x = x_ref[...]
        h = _rms(x, g_ref[...]).astype(BF16)
        for c in range(d_ff // FFN_CHUNK):
            lo = c * FFN_CHUNK
            gate = _dot(h, wup_bf[:, lo:lo + FFN_CHUNK])
            up = _dot(h, wup_bf[:, d_ff + lo:d_ff + lo + FFN_CHUNK])
            act_ref[:, lo:lo + FFN_CHUNK] = (jax.nn.silu(gate) * up).astype(BF16)
        y = x + 0.5 * _dot(act_ref[...], wdn_bf[...])
        if final_norm:
            y = _rms(y, fg_ref[...])
        o_ref[...] = y


def _ffn(x3, norm_rows, g_row, w_up, w_dn, layer, final_g, final_norm, cast=()):
    batch, seq, d = x3.shape
    d_ff = w_dn.shape[1]
    n_load = FFN_LOAD_STEPS
    bf16_rows = 2 * SUBLANES
    assert seq % FFN_ROWS == 0 and d_ff % FFN_CHUNK == 0
    assert d % (n_load * bf16_rows) == 0 and d_ff % (n_load * bf16_rows) == 0
    up_rows = d // n_load
    dn_rows = d_ff // n_load
    tiles_per_seq = seq // FFN_ROWS
    n_tiles = batch * tiles_per_seq

    def tile_index(i):
        return jnp.maximum(i - n_load, 0)

    def row_index(i):
        t = tile_index(i)
        return t // tiles_per_seq, t % tiles_per_seq, 0

    def load_index(i):
        return jnp.minimum(i, n_load - 1)

    cast_in_specs, cast_out_specs, cast_shapes, cast_bytes = [], [], [], 0
    for w, idx in cast:
        _, rows, cols = w.shape
        assert rows % (n_tiles * bf16_rows) == 0
        blk = rows // n_tiles
        cast_in_specs.append(pl.BlockSpec((None, blk, cols), lambda i, k=idx: (k, tile_index(i), 0)))
        cast_out_specs.append(pl.BlockSpec((blk, cols), lambda i: (tile_index(i), 0)))
        cast_shapes.append(jax.ShapeDtypeStruct((rows, cols), BF16))
        cast_bytes += 2 * blk * cols * (4 + 2)

    row_tile = pl.BlockSpec((None, FFN_ROWS, d), row_index)
    vmem = (4 * FFN_ROWS * d * 4
            + (d * 2 * d_ff + d_ff * d) * 2
            + 2 * (up_rows * 2 * d_ff + dn_rows * d) * 4
            + FFN_ROWS * d_ff * 2
            + cast_bytes
            + 4 * FFN_ROWS * d * 4)
    outs = pl.pallas_call(
        functools.partial(_ffn_kernel, final_norm=final_norm, n_cast=len(cast)),
        out_shape=(jax.ShapeDtypeStruct((batch, seq, d), F32), *cast_shapes),
        grid=(n_load + n_tiles,),
        in_specs=[
            row_tile,
            pl.BlockSpec((None, 1, d), lambda i: (g_row, 0, 0)),
            pl.BlockSpec((None, up_rows, 2 * d_ff), lambda i: (layer, load_index(i), 0)),
            pl.BlockSpec((None, dn_rows, d), lambda i: (layer, load_index(i), 0)),
            pl.BlockSpec((1, d), lambda i: (0, 0)),
            *cast_in_specs,
        ],
        out_specs=(row_tile, *cast_out_specs),
        scratch_shapes=[pltpu.VMEM((d, 2 * d_ff), BF16),
                        pltpu.VMEM((d_ff, d), BF16),
                        pltpu.VMEM((FFN_ROWS, d_ff), BF16)],
        compiler_params=_params(("arbitrary",), vmem),
        name="ffn_final" if final_norm else "ffn",
    )(x3, norm_rows, w_up, w_dn, final_g, *(w for w, _ in cast))
    return outs[0], outs[1:]


CONV_ROWS = 1024
CONV_SUBTILES = 2


def _conv_kernel(x_ref, g_ref, win_ref, cw_ref, wout_ref, o_ref, u_ref):
    rows, d = x_ref.shape
    sub = rows // CONV_SUBTILES

    @pl.when(pl.program_id(1) == 0)
    def _():
        u_ref[0:SUBLANES, :] = jnp.zeros((SUBLANES, d), F32)

    @pl.when(pl.program_id(1) > 0)
    def _():
        u_ref[0:SUBLANES, :] = u_ref[rows:rows + SUBLANES, :]

    w = cw_ref[...]
    for r in range(CONV_SUBTILES):
        lo = r * sub
        x = x_ref[lo:lo + sub, :]
        h = _rms(x, g_ref[...]).astype(BF16)
        cbv = _dot(h, win_ref[...])
        u = cbv[:, :d] * cbv[:, 2 * d:]
        b_gate = cbv[:, d:2 * d]
        u_ref[SUBLANES + lo:SUBLANES + lo + sub, :] = u
        conv = w[CONV_WIDTH - 1:CONV_WIDTH, :] * u
        for k in range(CONV_WIDTH - 1):
            first = SUBLANES + lo - (CONV_WIDTH - 1 - k)
            conv = conv + w[k:k + 1, :] * u_ref[first:first + sub, :]
        y = (b_gate * conv).astype(BF16)
        o_ref[lo:lo + sub, :] = x + _dot(y, wout_ref[...])


def _conv_mixer(x3, norm_rows, g_row, w_in, conv_w, w_out, j):
    batch, seq, d = x3.shape
    assert seq % CONV_ROWS == 0
    row_tile = pl.BlockSpec((None, CONV_ROWS, d), lambda b, l: (b, l, 0))
    vmem = (4 * CONV_ROWS * d * 4 + (3 * d * d + d * d) * 2
            + (CONV_ROWS + SUBLANES) * d * 4 + 8 * CONV_ROWS * d * 4)
    return pl.pallas_call(
        _conv_kernel,
        out_shape=jax.ShapeDtypeStruct((batch, seq, d), F32),
        grid=(batch, seq // CONV_ROWS),
        in_specs=[
            row_tile,
            pl.BlockSpec((None, 1, d), lambda b, l: (g_row, 0, 0)),
            _resident((d, 3 * d), lambda b, l: (0, 0)),
            pl.BlockSpec((None, CONV_WIDTH, d), lambda b, l: (j, 0, 0)),
            _resident((d, d), lambda b, l: (0, 0)),
        ],
        out_specs=row_tile,
        scratch_shapes=[pltpu.VMEM((CONV_ROWS + SUBLANES, d), F32)],
        compiler_params=_params(("arbitrary", "arbitrary"), vmem),
        name="conv_mixer",
    )(x3, norm_rows, w_in, conv_w, w_out)


XA_ROWS = 2048
XA_SUBTILES = 2


def _kv_kernel(m_ref, g_ref, wkv_ref, kt_ref, v_ref):
    d = m_ref.shape[1]
    mn = _rms(m_ref[...], g_ref[...]).astype(BF16)
    kv = _dot(mn, wkv_ref[...])
    kt_ref[...] = kv[:, :d].T.astype(BF16)
    v_ref[...] = kv[:, d:].astype(BF16)


def _memory_kv(mem, norm_rows, g_row, w_kv):
    batch, mlen, d = mem.shape
    vmem = 2 * mlen * d * 4 + 2 * d * 2 * d * 2 + 4 * mlen * d * 2 + 8 * mlen * d * 4
    return pl.pallas_call(
        _kv_kernel,
        out_shape=(jax.ShapeDtypeStruct((batch, d, mlen), BF16),
                   jax.ShapeDtypeStruct((batch, mlen, d), BF16)),
        grid=(batch,),
        in_specs=[
            pl.BlockSpec((None, mlen, d), lambda b: (b, 0, 0)),
            pl.BlockSpec((None, 1, d), lambda b: (g_row, 0, 0)),
            _resident((d, 2 * d), lambda b: (0, 0)),
        ],
        out_specs=(pl.BlockSpec((None, d, mlen), lambda b: (b, 0, 0)),
                   pl.BlockSpec((None, mlen, d), lambda b: (b, 0, 0))),
        compiler_params=_params(("parallel",), vmem),
        name="memory_kv",
    )(mem, norm_rows, w_kv)


def _xattn_kernel(x_ref, g_ref, wq_ref, kt_ref, v_ref, wo_ref, o_ref, cat_ref):
    d = x_ref.shape[1]
    hd = d // XA_HEADS
    scale = hd ** -0.5
    sub = x_ref.shape[0] // XA_SUBTILES
    for r in range(XA_SUBTILES):
        rs = slice(r * sub, (r + 1) * sub)
        x = x_ref[rs, :]
        h = _rms(x, g_ref[...]).astype(BF16)
        q = _dot(h, wq_ref[...]).astype(BF16)
        for a in range(XA_HEADS):
            lo = a * hd
            s = _dot(q[:, lo:lo + hd], kt_ref[lo:lo + hd, :]) * scale
            e = jnp.exp(s - jnp.max(s, axis=-1, keepdims=True))
            p = (e / jnp.sum(e, axis=-1, keepdims=True)).astype(BF16)
            cat_ref[rs, lo:lo + hd] = _dot(p, v_ref[:, lo:lo + hd]).astype(BF16)
        o_ref[rs, :] = x + _dot(cat_ref[rs, :], wo_ref[...])


def _cross_attention(x3, norm_rows, g_row, w_q, kt, v, w_o):
    batch, seq, d = x3.shape
    mlen = v.shape[1]
    assert seq % XA_ROWS == 0
    row_tile = pl.BlockSpec((None, XA_ROWS, d), lambda b, l: (b, l, 0))
    vmem = (4 * XA_ROWS * d * 4 + 2 * d * d * 2 + 4 * mlen * d * 2
            + XA_ROWS * d * 2 + 8 * XA_ROWS * d * 4)
    return pl.pallas_call(
        _xattn_kernel,
        out_shape=jax.ShapeDtypeStruct((batch, seq, d), F32),
        grid=(batch, seq // XA_ROWS),
        in_specs=[
            row_tile,
            pl.BlockSpec((None, 1, d), lambda b, l: (g_row, 0, 0)),
            _resident((d, d), lambda b, l: (0, 0)),
            pl.BlockSpec((None, d, mlen), lambda b, l: (b, 0, 0)),
            pl.BlockSpec((None, mlen, d), lambda b, l: (b, 0, 0)),
            _resident((d, d), lambda b, l: (0, 0)),
        ],
        out_specs=row_tile,
        scratch_shapes=[pltpu.VMEM((XA_ROWS, d), BF16)],
        compiler_params=_params(("parallel", "parallel"), vmem),
        name="cross_attention",
    )(x3, norm_rows, w_q, kt, v, w_o)


S5_STEPS = 32


def _s5_disc_kernel(are_ref, aim_ref, ldt_ref, b_ref, abr_ref, abi_ref, wb_ref):
    lam_re = jnp.minimum(are_ref[...], EIG_CLIP)
    lam_im = aim_ref[...]
    dt = jnp.exp(ldt_ref[...])
    mag = jnp.exp(lam_re * dt)
    ab_re = mag * jnp.cos(lam_im * dt)
    ab_im = mag * jnp.sin(lam_im * dt)
    abr_ref[...] = ab_re
    abi_ref[...] = ab_im
    den = lam_re * lam_re + lam_im * lam_im
    num_re = ab_re - 1.0
    num_im = ab_im
    coef_re = (num_re * lam_re + num_im * lam_im) / den
    coef_im = (num_im * lam_re - num_re * lam_im) / den
    for j in range(wb_ref.shape[0]):
        for q in range(PAIRS_PER_BLOCK):
            r = j * PAIRS_PER_BLOCK + q
            cr = coef_re[r:r + 1, :]
            ci = coef_im[r:r + 1, :]
            br = b_ref[0, j, :, q * LANES:(q + 1) * LANES]
            bi = b_ref[1, j, :, q * LANES:(q + 1) * LANES]
            lo = q * 2 * LANES
            wb_ref[j, :, lo:lo + LANES] = (cr * br - ci * bi).astype(BF16)
            wb_ref[j, :, lo + LANES:lo + 2 * LANES] = (cr * bi + ci * br).astype(BF16)


def _s5_prepare(a_re, a_im, log_dt, b_re, b_im, c_re, c_im):
    nl, g, p = a_re.shape
    h = b_re.shape[-1]
    nblk = g // GROUPS_PER_BLOCK
    npair = nblk * PAIRS_PER_BLOCK
    cols = lambda a: a.reshape(nl, npair, 2 * p)
    ldt = jnp.broadcast_to(log_dt[:, :, None], (nl, g, p))
    same_group = (np.arange(LANES)[:, None] // h) == (np.arange(GROUPS_PER_BLOCK * p)[None, :] // p)
    b = jnp.stack([b_re, b_im], axis=1).reshape(nl, 2, nblk, GROUPS_PER_BLOCK, p, h)
    b = b.transpose(0, 1, 2, 5, 3, 4).reshape(nl, 2, nblk, h, GROUPS_PER_BLOCK * p)
    b = jnp.where(same_group, jnp.tile(b, (1, 1, 1, GROUPS_PER_BLOCK, 1)), 0.0)
    row = np.arange(STATE_COLS)
    row_group = 2 * (row // (4 * p)) + (row // p) % 2
    same_group = row_group[:, None] == (np.arange(LANES)[None, :] // h)
    c = jnp.stack([c_re, -c_im], axis=1).reshape(nl, 2, nblk, PAIRS_PER_BLOCK, 2, h, p)
    c = c.transpose(0, 2, 3, 1, 4, 6, 5).reshape(nl, nblk, STATE_COLS, h)
    wc = jnp.where(same_group, jnp.tile(c, (1, 1, 1, GROUPS_PER_BLOCK)), 0.0).astype(BF16)
    par = pl.BlockSpec((None, npair, 2 * p), lambda i: (i, 0, 0))
    abar_re, abar_im, wb = pl.pallas_call(
        _s5_disc_kernel,
        out_shape=(jax.ShapeDtypeStruct((nl, npair, 2 * p), F32),) * 2
        + (jax.ShapeDtypeStruct((nl, nblk, LANES, STATE_COLS), BF16),),
        grid=(nl,),
        in_specs=[par, par, par,
                  pl.BlockSpec((None,) + b.shape[1:], lambda i: (i, 0, 0, 0, 0))],
        out_specs=(par, par,
                   pl.BlockSpec((None, nblk, LANES, STATE_COLS), lambda i: (i, 0, 0, 0))),
        compiler_params=_params(("parallel",), 16 * b[0].size * 4),
        name="s5_discretise",
    )(cols(a_re), cols(a_im), cols(ldt), b)
    return wb, wc, abar_re, abar_im


def _s5_kernel(x_ref, g_ref, wb_ref, wc_ref, abr_ref, abi_ref, dskip_ref, wglu_ref, o_ref,
               hn_ref, bu_ref, st_ref, z_ref, carry_ref):
    batch, steps, d = x_ref.shape
    rows = batch * steps
    nblk = d // LANES

    @pl.when(pl.program_id(0) == 0)
    def _():
        carry_ref[...] = jnp.zeros(carry_ref.shape, F32)

    x = x_ref[...].reshape(rows, d)
    hn = _rms(x, g_ref[...])
    for j in range(nblk):
        hn_ref[j] = hn[:, j * LANES:(j + 1) * LANES]

    for j in range(nblk):
        bu = bu_ref.at[j % 2]
        st = st_ref.at[j % 2]
        blk = hn_ref[j].reshape(batch, steps, LANES)
        hnt = jnp.swapaxes(blk, 0, 1).reshape(rows, LANES).astype(BF16)
        bu[...] = _dot(hnt, wb_ref[j])

        coef = []
        for q in range(PAIRS_PER_BLOCK):
            r = j * PAIRS_PER_BLOCK + q
            coef += [jnp.broadcast_to(abr_ref[r:r + 1, :], (batch, LANES)),
                     jnp.broadcast_to(abi_ref[r:r + 1, :], (batch, LANES))]
        state = [carry_ref[j, r] for r in range(2 * PAIRS_PER_BLOCK)]
        for t in range(steps):
            r0 = t * batch
            for q in range(PAIRS_PER_BLOCK):
                s_re, s_im = state[2 * q], state[2 * q + 1]
                a_re, a_im = coef[2 * q], coef[2 * q + 1]
                c_re = q * 2 * LANES
                c_im = c_re + LANES
                n_re = a_re * s_re - a_im * s_im + bu[r0:r0 + batch, c_re:c_re + LANES]
                n_im = a_re * s_im + a_im * s_re + bu[r0:r0 + batch, c_im:c_im + LANES]
                st[r0:r0 + batch, c_re:c_re + LANES] = n_re.astype(BF16)
                st[r0:r0 + batch, c_im:c_im + LANES] = n_im.astype(BF16)
                state[2 * q], state[2 * q + 1] = n_re, n_im
        for r in range(2 * PAIRS_PER_BLOCK):
            carry_ref[j, r] = state[r]

        yt = _dot(st[...], wc_ref[j])
        y = jnp.swapaxes(yt.reshape(steps, batch, LANES), 0, 1).reshape(rows, LANES)
        y = y + dskip_ref[:, j * LANES:(j + 1) * LANES] * hn_ref[j]
        z_ref[:, j * LANES:(j + 1) * LANES] = jax.nn.gelu(y).astype(BF16)

    glu = _dot(z_ref[...], wglu_ref[...])
    out = x_ref[...].reshape(rows, d) + glu[:, :d] * jax.nn.sigmoid(glu[:, d:])
    o_ref[...] = out.reshape(batch, steps, d)


def _s5_mixer(x3, norm_rows, g_row, wb, wc, abar_re, abar_im, d_skip, w_glu, j):
    batch, seq, d = x3.shape
    steps = S5_STEPS
    assert seq % steps == 0 and steps % SUBLANES == 0 and batch % (2 * SUBLANES) == 0
    rows = batch * steps
    nblk = d // LANES
    vmem = (4 * rows * d * 4
            + 2 * 2 * nblk * LANES * STATE_COLS * 2
            + 2 * d * d * 2
            + 2 * rows * d * 4 + rows * STATE_COLS * 6 + rows * d * 2
            + 8 * rows * d * 4)
    return pl.pallas_call(
        _s5_kernel,
        out_shape=jax.ShapeDtypeStruct((batch, seq, d), F32),
        grid=(seq // steps,),
        in_specs=[
            pl.BlockSpec((batch, steps, d), lambda c: (0, c, 0)),
            pl.BlockSpec((None, 1, d), lambda c: (g_row, 0, 0)),
            _resident((None, nblk, LANES, STATE_COLS), lambda c: (j, 0, 0, 0)),
            _resident((None, nblk, STATE_COLS, LANES), lambda c: (j, 0, 0, 0)),
            _resident((None, nblk * PAIRS_PER_BLOCK, LANES), lambda c: (j, 0, 0)),
            _resident((None, nblk * PAIRS_PER_BLOCK, LANES), lambda c: (j, 0, 0)),
            pl.BlockSpec((None, 1, d), lambda c: (j, 0, 0)),
            _resident((d, 2 * d), lambda c: (0, 0)),
        ],
        out_specs=pl.BlockSpec((batch, steps, d), lambda c: (0, c, 0)),
        scratch_shapes=[
            pltpu.VMEM((nblk, rows, LANES), F32),
            pltpu.VMEM((2, rows, STATE_COLS), F32),
            pltpu.VMEM((2, rows, STATE_COLS), BF16),
            pltpu.VMEM((rows, d), BF16),
            pltpu.VMEM((nblk, 2 * PAIRS_PER_BLOCK, batch, LANES), F32),
        ],
        compiler_params=_params(("arbitrary",), vmem),
        name="s5_mixer",
    )(x3, norm_rows, wb, wc, abar_re, abar_im, d_skip, w_glu)


def kernel(x, mem, norm_g, final_g, ffn1_up, ffn1_down, ffn2_up, ffn2_down,
           conv_w_in, conv_w, conv_w_out,
           ssm_a_re, ssm_a_im, ssm_log_dt, ssm_b_re, ssm_b_im,
           ssm_c_re, ssm_c_im, ssm_d, ssm_w_glu,
           xa_w_q, xa_w_kv, xa_w_o):
    batch, seq, d = x.shape
    depth = norm_g.shape[0]
    assert norm_g.shape[1] == N_NORMS
    assert ssm_b_re.shape[-1] == SSM_GROUP and ssm_a_re.shape[-1] == SSM_STATE

    norm_rows = norm_g.reshape(depth * N_NORMS, 1, d)
    final_row = final_g.reshape(1, d)
    s5 = _s5_prepare(ssm_a_re, ssm_a_im, ssm_log_dt, ssm_b_re, ssm_b_im, ssm_c_re, ssm_c_im)
    s5_d = ssm_d.reshape(ssm_d.shape[0], 1, d)

    for i in range(depth):
        row = i * N_NORMS
        j = i // 2
        mixer_w = ((conv_w_in, j), (conv_w_out, j)) if i % 2 == 0 else ((ssm_w_glu, j),)
        x, w_bf = _ffn(x, norm_rows, row + 0, ffn1_up, ffn1_down, i, final_row, False,
                       cast=mixer_w + ((xa_w_kv, i), (xa_w_q, i), (xa_w_o, i)))
        w_kv, w_q, w_o = w_bf[-3:]
        if i % 2 == 0:
            x = _conv_mixer(x, norm_rows, row + 1, w_bf[0], conv_w, w_bf[1], j)
        else:
            x = _s5_mixer(x, norm_rows, row + 1, *s5, s5_d, w_bf[0], j)
        kt, v = _memory_kv(mem, norm_rows, row + 3, w_kv)
        x = _cross_attention(x, norm_rows, row + 2, w_q, kt, v, w_o)
        x, _ = _ffn(x, norm_rows, row + 4, ffn2_up, ffn2_down, i, final_row, i == depth - 1)
    return x
```

```python
import functools
import math

import jax
import jax.numpy as jnp
import numpy as np
from jax import lax
from jax.experimental import pallas as pl
from jax.experimental.pallas import tpu as pltpu

F32 = jnp.float32
BF16 = jnp.bfloat16

NORM_EPS = 1e-6
EIG_CLIP = -1e-4
N_NORMS = 5
CONV_WIDTH = 3
SSM_GROUP = 16
SSM_STATE = 64
XA_HEADS = 4

LANES = 128
SUBLANES = 8
V7X_VMEM_BYTES = 64 * 1024 * 1024

GROUPS_PER_BLOCK = LANES // SSM_GROUP
PAIRS_PER_BLOCK = GROUPS_PER_BLOCK // 2
STATE_COLS = GROUPS_PER_BLOCK * 2 * SSM_STATE


def _rms(x, g):
    ms = jnp.mean(x * x, axis=-1, keepdims=True)
    return x * lax.rsqrt(ms + NORM_EPS) * g


def _dot(a, b):
    return jnp.dot(a, b, preferred_element_type=F32)


def _resident(block_shape, index_map):
    return pl.BlockSpec(block_shape, index_map, pipeline_mode=pl.Buffered(1))


def _params(semantics):
    return pltpu.CompilerParams(dimension_semantics=semantics,
                                vmem_limit_bytes=V7X_VMEM_BYTES)


FFN_ROWS = 1024
FFN_CHUNK = 256
FFN_LOAD_STEPS = 8


def _ffn_kernel(x_ref, g_ref, wup_ref, wdn_ref, fg_ref, *refs, final_norm, n_cast, n_load):
    cast_in = refs[:n_cast]
    o_ref = refs[n_cast]
    cast_out = refs[n_cast + 1:2 * n_cast + 1]
    scratch = refs[2 * n_cast + 1:]
    if n_load:
        wup_bf, wdn_bf, act_ref = scratch
    else:
        wup_bf, wdn_bf = wup_ref, wdn_ref
        (act_ref,) = scratch
    d_ff = wdn_bf.shape[0]

    def load_chunk():
        step = pl.program_id(0)
        up_rows = wup_ref.shape[0]
        dn_rows = wdn_ref.shape[0]
        first = pl.multiple_of(step * up_rows, up_rows)
        wup_bf[pl.ds(first, up_rows), :] = wup_ref[...].astype(BF16)
        first = pl.multiple_of(step * dn_rows, dn_rows)
        wdn_bf[pl.ds(first, dn_rows), :] = wdn_ref[...].astype(BF16)

    def row_tile():
        for src, dst in zip(cast_in, cast_out):
            dst[...] = src[...].astype(BF16)
        x = x_ref[...]
        h = _rms(x, g_ref[...]).astype(BF16)
        for c in range(d_ff // FFN_CHUNK):
            lo = c * FFN_CHUNK
            gate = _dot(h, wup_bf[:, lo:lo + FFN_CHUNK])
            up = _dot(h, wup_bf[:, d_ff + lo:d_ff + lo + FFN_CHUNK])
            act_ref[:, lo:lo + FFN_CHUNK] = (jax.nn.silu(gate) * up).astype(BF16)
        y = x + 0.5 * _dot(act_ref[...], wdn_bf[...])
        if final_norm:
            y = _rms(y, fg_ref[...])
        o_ref[...] = y

    if n_load:
        pl.when(pl.program_id(0) < n_load)(load_chunk)
        pl.when(pl.program_id(0) >= n_load)(row_tile)
    else:
        row_tile()


def _ffn(x3, norm_rows, g_row, w_up, w_dn, layer, final_g, final_norm, cast=()):
    batch, seq, d = x3.shape
    d_ff = w_dn.shape[-2]
    n_load = FFN_LOAD_STEPS if w_up.dtype == F32 else 0
    bf16_rows = 2 * SUBLANES
    assert seq % FFN_ROWS == 0 and d_ff % FFN_CHUNK == 0
    tiles_per_seq = seq // FFN_ROWS
    n_tiles = batch * tiles_per_seq

    def tile_index(i):
        return jnp.maximum(i - n_load, 0)

    def row_index(i):
        t = tile_index(i)
        return t // tiles_per_seq, t % tiles_per_seq, 0

    if n_load:
        assert d % (n_load * bf16_rows) == 0 and d_ff % (n_load * bf16_rows) == 0
        load_index = lambda i: jnp.minimum(i, n_load - 1)
        weight_specs = [
            pl.BlockSpec((None, d // n_load, 2 * d_ff), lambda i: (layer, load_index(i), 0)),
            pl.BlockSpec((None, d_ff // n_load, d), lambda i: (layer, load_index(i), 0)),
        ]
        weight_scratch = [pltpu.VMEM((d, 2 * d_ff), BF16), pltpu.VMEM((d_ff, d), BF16)]
    else:
        weight_specs = [_resident((d, 2 * d_ff), lambda i: (0, 0)),
                        _resident((d_ff, d), lambda i: (0, 0))]
        weight_scratch = []

    cast_in_specs, cast_out_specs, cast_shapes = [], [], []
    for w, idx in cast:
        _, rows, cols = w.shape
        n_blk = max(n for n in range(1, n_tiles + 1) if rows % (n * bf16_rows) == 0)
        blk = rows // n_blk
        block_index = lambda i, n=n_blk: jnp.minimum(tile_index(i), n - 1)
        cast_in_specs.append(
            pl.BlockSpec((None, blk, cols), lambda i, k=idx, b=block_index: (k, b(i), 0)))
        cast_out_specs.append(pl.BlockSpec((blk, cols), lambda i, b=block_index: (b(i), 0)))
        cast_shapes.append(jax.ShapeDtypeStruct((rows, cols), BF16))

    row_tile = pl.BlockSpec((None, FFN_ROWS, d), row_index)
    outs = pl.pallas_call(
        functools.partial(_ffn_kernel, final_norm=final_norm, n_cast=len(cast), n_load=n_load),
        out_shape=(jax.ShapeDtypeStruct((batch, seq, d), F32), *cast_shapes),
        grid=(n_load + n_tiles,),
        in_specs=[
            row_tile,
            pl.BlockSpec((None, 1, d), lambda i: (g_row, 0, 0)),
            *weight_specs,
            pl.BlockSpec((1, d), lambda i: (0, 0)),
            *cast_in_specs,
        ],
        out_specs=(row_tile, *cast_out_specs),
        scratch_shapes=[*weight_scratch, pltpu.VMEM((FFN_ROWS, d_ff), BF16)],
        compiler_params=_params(("arbitrary",)),
        name="ffn_final" if final_norm else "ffn",
    )(x3, norm_rows, w_up, w_dn, final_g, *(w for w, _ in cast))
    return outs[0], outs[1:]


CONV_ROWS = 1024
CONV_SUBTILES = 2


def _conv_kernel(x_ref, g_ref, win_ref, cw_ref, wout_ref, o_ref, u_ref):
    rows, d = x_ref.shape
    sub = rows // CONV_SUBTILES

    @pl.when(pl.program_id(1) == 0)
    def _():
        u_ref[0:SUBLANES, :] = jnp.zeros((SUBLANES, d), F32)

    @pl.when(pl.program_id(1) > 0)
    def _():
        u_ref[0:SUBLANES, :] = u_ref[rows:rows + SUBLANES, :]

    w = cw_ref[...]
    for r in range(CONV_SUBTILES):
        lo = r * sub
        x = x_ref[lo:lo + sub, :]
        h = _rms(x, g_ref[...]).astype(BF16)
        cbv = _dot(h, win_ref[...])
        u = cbv[:, :d] * cbv[:, 2 * d:]
        b_gate = cbv[:, d:2 * d]
        u_ref[SUBLANES + lo:SUBLANES + lo + sub, :] = u
        conv = w[CONV_WIDTH - 1:CONV_WIDTH, :] * u
        for k in range(CONV_WIDTH - 1):
            first = SUBLANES + lo - (CONV_WIDTH - 1 - k)
            conv = conv + w[k:k + 1, :] * u_ref[first:first + sub, :]
        y = (b_gate * conv).astype(BF16)
        o_ref[lo:lo + sub, :] = x + _dot(y, wout_ref[...])


def _conv_mixer(x3, norm_rows, g_row, w_in, conv_w, w_out, j):
    batch, seq, d = x3.shape
    assert seq % CONV_ROWS == 0
    row_tile = pl.BlockSpec((None, CONV_ROWS, d), lambda b, l: (b, l, 0))
    return pl.pallas_call(
        _conv_kernel,
        out_shape=jax.ShapeDtypeStruct((batch, seq, d), F32),
        grid=(batch, seq // CONV_ROWS),
        in_specs=[
            row_tile,
            pl.BlockSpec((None, 1, d), lambda b, l: (g_row, 0, 0)),
            _resident((d, 3 * d), lambda b, l: (0, 0)),
            pl.BlockSpec((None, CONV_WIDTH, d), lambda b, l: (j, 0, 0)),
            _resident((d, d), lambda b, l: (0, 0)),
        ],
        out_specs=row_tile,
        scratch_shapes=[pltpu.VMEM((CONV_ROWS + SUBLANES, d), F32)],
        compiler_params=_params(("arbitrary", "arbitrary")),
        name="conv_mixer",
    )(x3, norm_rows, w_in, conv_w, w_out)


XA_ROWS = 2048
XA_SUBTILES = 2


def _kv_kernel(m_ref, g_ref, wkv_ref, kt_ref, v_ref):
    d = m_ref.shape[1]
    mn = _rms(m_ref[...], g_ref[...]).astype(BF16)
    kv = _dot(mn, wkv_ref[...])
    kt_ref[...] = kv[:, :d].T.astype(BF16)
    v_ref[...] = kv[:, d:].astype(BF16)


def _memory_kv(mem, norm_rows, g_row, w_kv):
    batch, mlen, d = mem.shape
    return pl.pallas_call(
        _kv_kernel,
        out_shape=(jax.ShapeDtypeStruct((batch, d, mlen), BF16),
                   jax.ShapeDtypeStruct((batch, mlen, d), BF16)),
        grid=(batch,),
        in_specs=[
            pl.BlockSpec((None, mlen, d), lambda b: (b, 0, 0)),
            pl.BlockSpec((None, 1, d), lambda b: (g_row, 0, 0)),
            _resident((d, 2 * d), lambda b: (0, 0)),
        ],
        out_specs=(pl.BlockSpec((None, d, mlen), lambda b: (b, 0, 0)),
                   pl.BlockSpec((None, mlen, d), lambda b: (b, 0, 0))),
        compiler_params=_params(("parallel",)),
        name="memory_kv",
    )(mem, norm_rows, w_kv)


def _xattn_kernel(x_ref, g_ref, wq_ref, kt_ref, v_ref, wo_ref, o_ref, cat_ref):
    d = x_ref.shape[1]
    hd = d // XA_HEADS
    scale = hd ** -0.5
    sub = x_ref.shape[0] // XA_SUBTILES
    for r in range(XA_SUBTILES):
        rs = slice(r * sub, (r + 1) * sub)
        x = x_ref[rs, :]
        h = _rms(x, g_ref[...]).astype(BF16)
        q = _dot(h, wq_ref[...]).astype(BF16)
        for a in range(XA_HEADS):
            lo = a * hd
            s = _dot(q[:, lo:lo + hd], kt_ref[lo:lo + hd, :]) * scale
            e = jnp.exp(s - jnp.max(s, axis=-1, keepdims=True))
            p = (e / jnp.sum(e, axis=-1, keepdims=True)).astype(BF16)
            cat_ref[rs, lo:lo + hd] = _dot(p, v_ref[:, lo:lo + hd]).astype(BF16)
        o_ref[rs, :] = x + _dot(cat_ref[rs, :], wo_ref[...])


def _cross_attention(x3, norm_rows, g_row, w_q, kt, v, w_o):
    batch, seq, d = x3.shape
    mlen = v.shape[1]
    assert seq % XA_ROWS == 0
    row_tile = pl.BlockSpec((None, XA_ROWS, d), lambda b, l: (b, l, 0))
    return pl.pallas_call(
        _xattn_kernel,
        out_shape=jax.ShapeDtypeStruct((batch, seq, d), F32),
        grid=(batch, seq // XA_ROWS),
        in_specs=[
            row_tile,
            pl.BlockSpec((None, 1, d), lambda b, l: (g_row, 0, 0)),
            _resident((d, d), lambda b, l: (0, 0)),
            pl.BlockSpec((None, d, mlen), lambda b, l: (b, 0, 0)),
            pl.BlockSpec((None, mlen, d), lambda b, l: (b, 0, 0)),
            _resident((d, d), lambda b, l: (0, 0)),
        ],
        out_specs=row_tile,
        scratch_shapes=[pltpu.VMEM((XA_ROWS, d), BF16)],
        compiler_params=_params(("parallel", "parallel")),
        name="cross_attention",
    )(x3, norm_rows, w_q, kt, v, w_o)


S5_STEPS = 32


def _s5_disc_kernel(are_ref, aim_ref, ldt_ref, b_ref, abr_ref, abi_ref, wb_ref):
    lam_re = jnp.minimum(are_ref[...], EIG_CLIP)
    lam_im = aim_ref[...]
    dt = jnp.exp(ldt_ref[...])
    mag = jnp.exp(lam_re * dt)
    ab_re = mag * jnp.cos(lam_im * dt)
    ab_im = mag * jnp.sin(lam_im * dt)
    abr_ref[...] = ab_re
    abi_ref[...] = ab_im
    den = lam_re * lam_re + lam_im * lam_im
    num_re = ab_re - 1.0
    num_im = ab_im
    coef_re = (num_re * lam_re + num_im * lam_im) / den
    coef_im = (num_im * lam_re - num_re * lam_im) / den
    for j in range(wb_ref.shape[0]):
        for q in range(PAIRS_PER_BLOCK):
            r = j * PAIRS_PER_BLOCK + q
            cr = coef_re[r:r + 1, :]
            ci = coef_im[r:r + 1, :]
            br = b_ref[0, j, :, q * LANES:(q + 1) * LANES]
            bi = b_ref[1, j, :, q * LANES:(q + 1) * LANES]
            lo = q * 2 * LANES
            wb_ref[j, :, lo:lo + LANES] = (cr * br - ci * bi).astype(BF16)
            wb_ref[j, :, lo + LANES:lo + 2 * LANES] = (cr * bi + ci * br).astype(BF16)


def _s5_prepare(a_re, a_im, log_dt, b_re, b_im, c_re, c_im):
    nl, g, p = a_re.shape
    h = b_re.shape[-1]
    nblk = g // GROUPS_PER_BLOCK
    npair = nblk * PAIRS_PER_BLOCK
    cols = lambda a: a.reshape(nl, npair, 2 * p)
    ldt = jnp.broadcast_to(log_dt[:, :, None], (nl, g, p))
    same_group = (np.arange(LANES)[:, None] // h) == (np.arange(GROUPS_PER_BLOCK * p)[None, :] // p)
    b = jnp.stack([b_re, b_im], axis=1).reshape(nl, 2, nblk, GROUPS_PER_BLOCK, p, h)
    b = b.transpose(0, 1, 2, 5, 3, 4).reshape(nl, 2, nblk, h, GROUPS_PER_BLOCK * p)
    b = jnp.where(same_group, jnp.tile(b, (1, 1, 1, GROUPS_PER_BLOCK, 1)), 0.0)
    row = np.arange(STATE_COLS)
    row_group = 2 * (row // (4 * p)) + (row // p) % 2
    same_group = row_group[:, None] == (np.arange(LANES)[None, :] // h)
    c = jnp.stack([c_re, -c_im], axis=1).reshape(nl, 2, nblk, PAIRS_PER_BLOCK, 2, h, p)
    c = c.transpose(0, 2, 3, 1, 4, 6, 5).reshape(nl, nblk, STATE_COLS, h)
    wc = jnp.where(same_group, jnp.tile(c, (1, 1, 1, GROUPS_PER_BLOCK)), 0.0).astype(BF16)
    par = pl.BlockSpec((None, npair, 2 * p), lambda i: (i, 0, 0))
    abar_re, abar_im, wb = pl.pallas_call(
        _s5_disc_kernel,
        out_shape=(jax.ShapeDtypeStruct((nl, npair, 2 * p), F32),) * 2
        + (jax.ShapeDtypeStruct((nl, nblk, LANES, STATE_COLS), BF16),),
        grid=(nl,),
        in_specs=[par, par, par,
                  pl.BlockSpec((None,) + b.shape[1:], lambda i: (i, 0, 0, 0, 0))],
        out_specs=(par, par,
                   pl.BlockSpec((None, nblk, LANES, STATE_COLS), lambda i: (i, 0, 0, 0))),
        compiler_params=_params(("parallel",)),
        name="s5_discretise",
    )(cols(a_re), cols(a_im), cols(ldt), b)
    return wb, wc, abar_re, abar_im


def _s5_kernel(x_ref, g_ref, wb_ref, wc_ref, abr_ref, abi_ref, dskip_ref, wglu_ref, o_ref,
               hn_ref, bu_ref, st_ref, z_ref, carry_ref):
    batch, steps, d = x_ref.shape
    rows = batch * steps
    nblk = d // LANES

    @pl.when(pl.program_id(0) == 0)
    def _():
        carry_ref[...] = jnp.zeros(carry_ref.shape, F32)

    x = x_ref[...].reshape(rows, d)
    hn = _rms(x, g_ref[...])
    for j in range(nblk):
        hn_ref[j] = hn[:, j * LANES:(j + 1) * LANES]

    for j in range(nblk):
        bu = bu_ref.at[j % 2]
        st = st_ref.at[j % 2]
        blk = hn_ref[j].reshape(batch, steps, LANES)
        hnt = jnp.swapaxes(blk, 0, 1).reshape(rows, LANES).astype(BF16)
        bu[...] = _dot(hnt, wb_ref[j])

        coef = []
        for q in range(PAIRS_PER_BLOCK):
            r = j * PAIRS_PER_BLOCK + q
            coef += [jnp.broadcast_to(abr_ref[r:r + 1, :], (batch, LANES)),
                     jnp.broadcast_to(abi_ref[r:r + 1, :], (batch, LANES))]
        state = [carry_ref[j, r] for r in range(2 * PAIRS_PER_BLOCK)]
        for t in range(steps):
            r0 = t * batch
            for q in range(PAIRS_PER_BLOCK):
                s_re, s_im = state[2 * q], state[2 * q + 1]
                a_re, a_im = coef[2 * q], coef[2 * q + 1]
                c_re = q * 2 * LANES
                c_im = c_re + LANES
                n_re = a_re * s_re - a_im * s_im + bu[r0:r0 + batch, c_re:c_re + LANES]
                n_im = a_re * s_im + a_im * s_re + bu[r0:r0 + batch, c_im:c_im + LANES]
                st[r0:r0 + batch, c_re:c_re + LANES] = n_re.astype(BF16)
                st[r0:r0 + batch, c_im:c_im + LANES] = n_im.astype(BF16)
                state[2 * q], state[2 * q + 1] = n_re, n_im
        for r in range(2 * PAIRS_PER_BLOCK):
            carry_ref[j, r] = state[r]

        yt = _dot(st[...], wc_ref[j])
        y = jnp.swapaxes(yt.reshape(steps, batch, LANES), 0, 1).reshape(rows, LANES)
        y = y + dskip_ref[:, j * LANES:(j + 1) * LANES] * hn_ref[j]
        z_ref[:, j * LANES:(j + 1) * LANES] = jax.nn.gelu(y).astype(BF16)

    glu = _dot(z_ref[...], wglu_ref[...])
    out = x_ref[...].reshape(rows, d) + glu[:, :d] * jax.nn.sigmoid(glu[:, d:])
    o_ref[...] = out.reshape(batch, steps, d)


def _s5_mixer(x3, norm_rows, g_row, wb, wc, abar_re, abar_im, d_skip, w_glu, j):
    batch, seq, d = x3.shape
    steps = S5_STEPS
    assert seq % steps == 0 and steps % SUBLANES == 0 and batch % (2 * SUBLANES) == 0
    rows = batch * steps
    nblk = d // LANES
    return pl.pallas_call(
        _s5_kernel,
        out_shape=jax.ShapeDtypeStruct((batch, seq, d), F32),
        grid=(seq // steps,),
        in_specs=[
            pl.BlockSpec((batch, steps, d), lambda c: (0, c, 0)),
            pl.BlockSpec((None, 1, d), lambda c: (g_row, 0, 0)),
            _resident((None, nblk, LANES, STATE_COLS), lambda c: (j, 0, 0, 0)),
            _resident((None, nblk, STATE_COLS, LANES), lambda c: (j, 0, 0, 0)),
            _resident((None, nblk * PAIRS_PER_BLOCK, LANES), lambda c: (j, 0, 0)),
            _resident((None, nblk * PAIRS_PER_BLOCK, LANES), lambda c: (j, 0, 0)),
            pl.BlockSpec((None, 1, d), lambda c: (j, 0, 0)),
            _resident((d, 2 * d), lambda c: (0, 0)),
        ],
        out_specs=pl.BlockSpec((batch, steps, d), lambda c: (0, c, 0)),
        scratch_shapes=[
            pltpu.VMEM((nblk, rows, LANES), F32),
            pltpu.VMEM((2, rows, STATE_COLS), F32),
            pltpu.VMEM((2, rows, STATE_COLS), BF16),
            pltpu.VMEM((rows, d), BF16),
            pltpu.VMEM((nblk, 2 * PAIRS_PER_BLOCK, batch, LANES), F32),
        ],
        compiler_params=_params(("arbitrary",)),
        name="s5_mixer",
    )(x3, norm_rows, wb, wc, abar_re, abar_im, d_skip, w_glu)


def kernel(x, mem, norm_g, final_g, ffn1_up, ffn1_down, ffn2_up, ffn2_down,
           conv_w_in, conv_w, conv_w_out,
           ssm_a_re, ssm_a_im, ssm_log_dt, ssm_b_re, ssm_b_im,
           ssm_c_re, ssm_c_im, ssm_d, ssm_w_glu,
           xa_w_q, xa_w_kv, xa_w_o):
    batch, seq, d = x.shape
    depth = norm_g.shape[0]
    assert norm_g.shape[1] == N_NORMS
    assert ssm_b_re.shape[-1] == SSM_GROUP and ssm_a_re.shape[-1] == SSM_STATE

    norm_rows = norm_g.reshape(depth * N_NORMS, 1, d)
    final_row = final_g.reshape(1, d)
    s5 = _s5_prepare(ssm_a_re, ssm_a_im, ssm_log_dt, ssm_b_re, ssm_b_im, ssm_c_re, ssm_c_im)
    s5_d = ssm_d.reshape(ssm_d.shape[0], 1, d)

    ffn1_w = (ffn1_up, ffn1_down)
    for i in range(depth):
        row = i * N_NORMS
        j = i // 2
        mixer_w = ((conv_w_in, j), (conv_w_out, j)) if i % 2 == 0 else ((ssm_w_glu, j),)
        x, w_bf = _ffn(x, norm_rows, row + 0, *ffn1_w, i, final_row, False,
                       cast=mixer_w + ((xa_w_kv, i), (xa_w_q, i), (xa_w_o, i),
                                       (ffn2_up, i), (ffn2_down, i)))
        w_kv, w_q, w_o, up2, down2 = w_bf[-5:]
        if i % 2 == 0:
            x = _conv_mixer(x, norm_rows, row + 1, w_bf[0], conv_w, w_bf[1], j)
        else:
            x = _s5_mixer(x, norm_rows, row + 1, *s5, s5_d, w_bf[0], j)
        kt, v = _memory_kv(mem, norm_rows, row + 3, w_kv)
        x = _cross_attention(x, norm_rows, row + 2, w_q, kt, v, w_o)
        last = i == depth - 1
        x, ffn1_w = _ffn(x, norm_rows, row + 4, up2, down2, i, final_row, last,
                         cast=() if last else ((ffn1_up, i + 1), (ffn1_down, i + 1)))
    return x
```

```python
import functools
import math

import jax
import jax.numpy as jnp
import numpy as np
from jax import lax
from jax.experimental import pallas as pl
from jax.experimental.pallas import tpu as pltpu

F32 = jnp.float32
BF16 = jnp.bfloat16

NORM_EPS = 1e-6
EIG_CLIP = -1e-4
N_NORMS = 5
CONV_WIDTH = 3
SSM_GROUP = 16
SSM_STATE = 64
XA_HEADS = 4

LANES = 128
SUBLANES = 8
V7X_VMEM_BYTES = 64 * 1024 * 1024

GROUPS_PER_BLOCK = LANES // SSM_GROUP
PAIRS_PER_BLOCK = GROUPS_PER_BLOCK // 2
STATE_COLS = GROUPS_PER_BLOCK * 2 * SSM_STATE


def _rms(x, g):
    ms = jnp.mean(x * x, axis=-1, keepdims=True)
    return x * lax.rsqrt(ms + NORM_EPS) * g


def _dot(a, b):
    return jnp.dot(a, b, preferred_element_type=F32)


def _resident(block_shape, index_map):
    return pl.BlockSpec(block_shape, index_map, pipeline_mode=pl.Buffered(1))


def _params(semantics):
    return pltpu.CompilerParams(dimension_semantics=semantics,
                                vmem_limit_bytes=V7X_VMEM_BYTES)


FFN_ROWS = 1024
FFN_CHUNK = 256
FFN_LOAD_STEPS = 8


def _ffn_kernel(x_ref, g_ref, wup_ref, wdn_ref, fg_ref, *refs, final_norm, n_cast, n_load):
    cast_in = refs[:n_cast]
    o_ref = refs[n_cast]
    cast_out = refs[n_cast + 1:2 * n_cast + 1]
    scratch = refs[2 * n_cast + 1:]
    if n_load:
        wup_bf, wdn_bf, act_ref = scratch
    else:
        wup_bf, wdn_bf = wup_ref, wdn_ref
        (act_ref,) = scratch
    d_ff = wdn_bf.shape[0]

    def load_chunk():
        step = pl.program_id(0)
        up_rows = wup_ref.shape[0]
        dn_rows = wdn_ref.shape[0]
        first = pl.multiple_of(step * up_rows, up_rows)
        wup_bf[pl.ds(first, up_rows), :] = wup_ref[...].astype(BF16)
        first = pl.multiple_of(step * dn_rows, dn_rows)
        wdn_bf[pl.ds(first, dn_rows), :] = wdn_ref[...].astype(BF16)

    def row_tile():
        for src, dst in zip(cast_in, cast_out):
            dst[...] = src[...].astype(BF16)
        x = x_ref[...]
        h = _rms(x, g_ref[...]).astype(BF16)
        for c in range(d_ff // FFN_CHUNK):
            lo = c * FFN_CHUNK
            gate = _dot(h, wup_bf[:, lo:lo + FFN_CHUNK])
            up = _dot(h, wup_bf[:, d_ff + lo:d_ff + lo + FFN_CHUNK])
            act_ref[:, lo:lo + FFN_CHUNK] = (jax.nn.silu(gate) * up).astype(BF16)
        y = x + 0.5 * _dot(act_ref[...], wdn_bf[...])
        if final_norm:
            y = _rms(y, fg_ref[...])
        o_ref[...] = y

    if n_load:
        pl.when(pl.program_id(0) < n_load)(load_chunk)
        pl.when(pl.program_id(0) >= n_load)(row_tile)
    else:
        row_tile()


def _ffn(x3, norm_rows, g_row, w_up, w_dn, layer, final_g, final_norm, cast=()):
    batch, seq, d = x3.shape
    d_ff = w_dn.shape[-2]
    n_load = FFN_LOAD_STEPS if w_up.dtype == F32 else 0
    bf16_rows = 2 * SUBLANES
    assert seq % FFN_ROWS == 0 and d_ff % FFN_CHUNK == 0
    tiles_per_seq = seq // FFN_ROWS
    n_tiles = batch * tiles_per_seq

    def tile_index(i):
        return jnp.maximum(i - n_load, 0)

    def row_index(i):
        t = tile_index(i)
        return t // tiles_per_seq, t % tiles_per_seq, 0

    if n_load:
        assert d % (n_load * bf16_rows) == 0 and d_ff % (n_load * bf16_rows) == 0
        load_index = lambda i: jnp.minimum(i, n_load - 1)
        weight_specs = [
            pl.BlockSpec((None, d // n_load, 2 * d_ff), lambda i: (layer, load_index(i), 0)),
            pl.BlockSpec((None, d_ff // n_load, d), lambda i: (layer, load_index(i), 0)),
        ]
        weight_scratch = [pltpu.VMEM((d, 2 * d_ff), BF16), pltpu.VMEM((d_ff, d), BF16)]
    else:
        weight_specs = [_resident((d, 2 * d_ff), lambda i: (0, 0)),
                        _resident((d_ff, d), lambda i: (0, 0))]
        weight_scratch = []

    cast_in_specs, cast_out_specs, cast_shapes = [], [], []
    for w, idx in cast:
        _, rows, cols = w.shape
        n_blk = max(n for n in range(1, n_tiles + 1) if rows % (n * bf16_rows) == 0)
        blk = rows // n_blk
        block_index = lambda i, n=n_blk: jnp.minimum(tile_index(i), n - 1)
        cast_in_specs.append(
            pl.BlockSpec((None, blk, cols), lambda i, k=idx, b=block_index: (k, b(i), 0)))
        cast_out_specs.append(pl.BlockSpec((blk, cols), lambda i, b=block_index: (b(i), 0)))
        cast_shapes.append(jax.ShapeDtypeStruct((rows, cols), BF16))

    row_tile = pl.BlockSpec((None, FFN_ROWS, d), row_index)
    outs = pl.pallas_call(
        functools.partial(_ffn_kernel, final_norm=final_norm, n_cast=len(cast), n_load=n_load),
        out_shape=(jax.ShapeDtypeStruct((batch, seq, d), F32), *cast_shapes),
        grid=(n_load + n_tiles,),
        in_specs=[
            row_tile,
            pl.BlockSpec((None, 1, d), lambda i: (g_row, 0, 0)),
            *weight_specs,
            pl.BlockSpec((1, d), lambda i: (0, 0)),
            *cast_in_specs,
        ],
        out_specs=(row_tile, *cast_out_specs),
        scratch_shapes=[*weight_scratch, pltpu.VMEM((FFN_ROWS, d_ff), BF16)],
        compiler_params=_params(("arbitrary",)),
        name="ffn_final" if final_norm else "ffn",
    )(x3, norm_rows, w_up, w_dn, final_g, *(w for w, _ in cast))
    return outs[0], outs[1:]


CONV_ROWS = 1024
CONV_SUBTILES = 2


def _conv_kernel(x_ref, g_ref, win_ref, cw_ref, wout_ref, o_ref, u_ref):
    rows, d = x_ref.shape
    sub = rows // CONV_SUBTILES

    @pl.when(pl.program_id(1) == 0)
    def _():
        u_ref[0:SUBLANES, :] = jnp.zeros((SUBLANES, d), F32)

    @pl.when(pl.program_id(1) > 0)
    def _():
        u_ref[0:SUBLANES, :] = u_ref[rows:rows + SUBLANES, :]

    w = cw_ref[...]
    for r in range(CONV_SUBTILES):
        lo = r * sub
        x = x_ref[lo:lo + sub, :]
        h = _rms(x, g_ref[...]).astype(BF16)
        cbv = _dot(h, win_ref[...])
        u = cbv[:, :d] * cbv[:, 2 * d:]
        b_gate = cbv[:, d:2 * d]
        u_ref[SUBLANES + lo:SUBLANES + lo + sub, :] = u
        conv = w[CONV_WIDTH - 1:CONV_WIDTH, :] * u
        for k in range(CONV_WIDTH - 1):
            first = SUBLANES + lo - (CONV_WIDTH - 1 - k)
            conv = conv + w[k:k + 1, :] * u_ref[first:first + sub, :]
        y = (b_gate * conv).astype(BF16)
        o_ref[lo:lo + sub, :] = x + _dot(y, wout_ref[...])


def _conv_mixer(x3, norm_rows, g_row, w_in, conv_w, w_out, j):
    batch, seq, d = x3.shape
    assert seq % CONV_ROWS == 0
    row_tile = pl.BlockSpec((None, CONV_ROWS, d), lambda b, l: (b, l, 0))
    return pl.pallas_call(
        _conv_kernel,
        out_shape=jax.ShapeDtypeStruct((batch, seq, d), F32),
        grid=(batch, seq // CONV_ROWS),
        in_specs=[
            row_tile,
            pl.BlockSpec((None, 1, d), lambda b, l: (g_row, 0, 0)),
            _resident((d, 3 * d), lambda b, l: (0, 0)),
            pl.BlockSpec((None, CONV_WIDTH, d), lambda b, l: (j, 0, 0)),
            _resident((d, d), lambda b, l: (0, 0)),
        ],
        out_specs=row_tile,
        scratch_shapes=[pltpu.VMEM((CONV_ROWS + SUBLANES, d), F32)],
        compiler_params=_params(("arbitrary", "arbitrary")),
        name="conv_mixer",
    )(x3, norm_rows, w_in, conv_w, w_out)


XA_ROWS = 2048
XA_SUBTILES = 2


KV_BATCH = 4


def _kv_kernel(m_ref, g_ref, wkv_ref, kt_ref, v_ref):
    nb, mlen, d = m_ref.shape
    mn = _rms(m_ref[...].reshape(nb * mlen, d), g_ref[...]).astype(BF16)
    kv = _dot(mn, wkv_ref[...])
    for b in range(nb):
        rows = kv[b * mlen:(b + 1) * mlen, :]
        kt_ref[b] = rows[:, :d].T.astype(BF16)
        v_ref[b] = rows[:, d:].astype(BF16)


def _memory_kv(mem, norm_rows, g_row, w_kv):
    batch, mlen, d = mem.shape
    assert batch % KV_BATCH == 0
    return pl.pallas_call(
        _kv_kernel,
        out_shape=(jax.ShapeDtypeStruct((batch, d, mlen), BF16),
                   jax.ShapeDtypeStruct((batch, mlen, d), BF16)),
        grid=(batch // KV_BATCH,),
        in_specs=[
            pl.BlockSpec((KV_BATCH, mlen, d), lambda b: (b, 0, 0)),
            pl.BlockSpec((None, 1, d), lambda b: (g_row, 0, 0)),
            _resident((d, 2 * d), lambda b: (0, 0)),
        ],
        out_specs=(pl.BlockSpec((KV_BATCH, d, mlen), lambda b: (b, 0, 0)),
                   pl.BlockSpec((KV_BATCH, mlen, d), lambda b: (b, 0, 0))),
        compiler_params=_params(("parallel",)),
        name="memory_kv",
    )(mem, norm_rows, w_kv)


def _xattn_kernel(x_ref, g_ref, wq_ref, kt_ref, v_ref, wo_ref, o_ref, cat_ref):
    d = x_ref.shape[1]
    hd = d // XA_HEADS
    scale = hd ** -0.5
    sub = x_ref.shape[0] // XA_SUBTILES
    for r in range(XA_SUBTILES):
        rs = slice(r * sub, (r + 1) * sub)
        x = x_ref[rs, :]
        h = _rms(x, g_ref[...]).astype(BF16)
        q = _dot(h, wq_ref[...]).astype(BF16)
        for a in range(XA_HEADS):
            lo = a * hd
            s = _dot(q[:, lo:lo + hd], kt_ref[lo:lo + hd, :]) * scale
            e = jnp.exp(s - jnp.max(s, axis=-1, keepdims=True))
            p = (e / jnp.sum(e, axis=-1, keepdims=True)).astype(BF16)
            cat_ref[rs, lo:lo + hd] = _dot(p, v_ref[:, lo:lo + hd]).astype(BF16)
        o_ref[rs, :] = x + _dot(cat_ref[rs, :], wo_ref[...])


def _cross_attention(x3, norm_rows, g_row, w_q, kt, v, w_o):
    batch, seq, d = x3.shape
    mlen = v.shape[1]
    assert seq % XA_ROWS == 0
    row_tile = pl.BlockSpec((None, XA_ROWS, d), lambda b, l: (b, l, 0))
    return pl.pallas_call(
        _xattn_kernel,
        out_shape=jax.ShapeDtypeStruct((batch, seq, d), F32),
        grid=(batch, seq // XA_ROWS),
        in_specs=[
            row_tile,
            pl.BlockSpec((None, 1, d), lambda b, l: (g_row, 0, 0)),
            _resident((d, d), lambda b, l: (0, 0)),
            pl.BlockSpec((None, d, mlen), lambda b, l: (b, 0, 0)),
            pl.BlockSpec((None, mlen, d), lambda b, l: (b, 0, 0)),
            _resident((d, d), lambda b, l: (0, 0)),
        ],
        out_specs=row_tile,
        scratch_shapes=[pltpu.VMEM((XA_ROWS, d), BF16)],
        compiler_params=_params(("parallel", "parallel")),
        name="cross_attention",
    )(x3, norm_rows, w_q, kt, v, w_o)


S5_STEPS = 32


def _s5_disc_kernel(are_ref, aim_ref, ldt_ref, b_ref, abr_ref, abi_ref, wb_ref):
    lam_re = jnp.minimum(are_ref[...], EIG_CLIP)
    lam_im = aim_ref[...]
    dt = jnp.exp(ldt_ref[...])
    mag = jnp.exp(lam_re * dt)
    ab_re = mag * jnp.cos(lam_im * dt)
    ab_im = mag * jnp.sin(lam_im * dt)
    abr_ref[...] = ab_re
    abi_ref[...] = ab_im
    den = lam_re * lam_re + lam_im * lam_im
    num_re = ab_re - 1.0
    num_im = ab_im
    coef_re = (num_re * lam_re + num_im * lam_im) / den
    coef_im = (num_im * lam_re - num_re * lam_im) / den
    for j in range(wb_ref.shape[0]):
        for q in range(PAIRS_PER_BLOCK):
            r = j * PAIRS_PER_BLOCK + q
            cr = coef_re[r:r + 1, :]
            ci = coef_im[r:r + 1, :]
            br = b_ref[0, j, :, q * LANES:(q + 1) * LANES]
            bi = b_ref[1, j, :, q * LANES:(q + 1) * LANES]
            lo = q * 2 * LANES
            wb_ref[j, :, lo:lo + LANES] = (cr * br - ci * bi).astype(BF16)
            wb_ref[j, :, lo + LANES:lo + 2 * LANES] = (cr * bi + ci * br).astype(BF16)


def _s5_prepare(a_re, a_im, log_dt, b_re, b_im, c_re, c_im):
    nl, g, p = a_re.shape
    h = b_re.shape[-1]
    nblk = g // GROUPS_PER_BLOCK
    npair = nblk * PAIRS_PER_BLOCK
    cols = lambda a: a.reshape(nl, npair, 2 * p)
    ldt = jnp.broadcast_to(log_dt[:, :, None], (nl, g, p))
    same_group = (np.arange(LANES)[:, None] // h) == (np.arange(GROUPS_PER_BLOCK * p)[None, :] // p)
    b = jnp.stack([b_re, b_im], axis=1).reshape(nl, 2, nblk, GROUPS_PER_BLOCK, p, h)
    b = b.transpose(0, 1, 2, 5, 3, 4).reshape(nl, 2, nblk, h, GROUPS_PER_BLOCK * p)
    b = jnp.where(same_group, jnp.tile(b, (1, 1, 1, GROUPS_PER_BLOCK, 1)), 0.0)
    row = np.arange(STATE_COLS)
    row_group = 2 * (row // (4 * p)) + (row // p) % 2
    same_group = row_group[:, None] == (np.arange(LANES)[None, :] // h)
    c = jnp.stack([c_re, -c_im], axis=1).reshape(nl, 2, nblk, PAIRS_PER_BLOCK, 2, h, p)
    c = c.transpose(0, 2, 3, 1, 4, 6, 5).reshape(nl, nblk, STATE_COLS, h)
    wc = jnp.where(same_group, jnp.tile(c, (1, 1, 1, GROUPS_PER_BLOCK)), 0.0).astype(BF16)
    par = pl.BlockSpec((None, npair, 2 * p), lambda i: (i, 0, 0))
    abar_re, abar_im, wb = pl.pallas_call(
        _s5_disc_kernel,
        out_shape=(jax.ShapeDtypeStruct((nl, npair, 2 * p), F32),) * 2
        + (jax.ShapeDtypeStruct((nl, nblk, LANES, STATE_COLS), BF16),),
        grid=(nl,),
        in_specs=[par, par, par,
                  pl.BlockSpec((None,) + b.shape[1:], lambda i: (i, 0, 0, 0, 0))],
        out_specs=(par, par,
                   pl.BlockSpec((None, nblk, LANES, STATE_COLS), lambda i: (i, 0, 0, 0))),
        compiler_params=_params(("parallel",)),
        name="s5_discretise",
    )(cols(a_re), cols(a_im), cols(ldt), b)
    return wb, wc, abar_re, abar_im


def _s5_kernel(x_ref, g_ref, wb_ref, wc_ref, abr_ref, abi_ref, dskip_ref, wglu_ref, o_ref,
               hn_ref, bu_ref, st_ref, z_ref, carry_ref):
    batch, steps, d = x_ref.shape
    rows = batch * steps
    nblk = d // LANES

    @pl.when(pl.program_id(0) == 0)
    def _():
        carry_ref[...] = jnp.zeros(carry_ref.shape, F32)

    x = x_ref[...].reshape(rows, d)
    hn = _rms(x, g_ref[...])
    for j in range(nblk):
        hn_ref[j] = hn[:, j * LANES:(j + 1) * LANES]

    for j in range(nblk):
        bu = bu_ref.at[j % 2]
        st = st_ref.at[j % 2]
        blk = hn_ref[j].reshape(batch, steps, LANES)
        hnt = jnp.swapaxes(blk, 0, 1).reshape(rows, LANES).astype(BF16)
        bu[...] = _dot(hnt, wb_ref[j])

        coef = []
        for q in range(PAIRS_PER_BLOCK):
            r = j * PAIRS_PER_BLOCK + q
            coef += [jnp.broadcast_to(abr_ref[r:r + 1, :], (batch, LANES)),
                     jnp.broadcast_to(abi_ref[r:r + 1, :], (batch, LANES))]
        state = [carry_ref[j, r] for r in range(2 * PAIRS_PER_BLOCK)]
        for t in range(steps):
            r0 = t * batch
            for q in range(PAIRS_PER_BLOCK):
                s_re, s_im = state[2 * q], state[2 * q + 1]
                a_re, a_im = coef[2 * q], coef[2 * q + 1]
                c_re = q * 2 * LANES
                c_im = c_re + LANES
                n_re = a_re * s_re - a_im * s_im + bu[r0:r0 + batch, c_re:c_re + LANES]
                n_im = a_re * s_im + a_im * s_re + bu[r0:r0 + batch, c_im:c_im + LANES]
                st[r0:r0 + batch, c_re:c_re + LANES] = n_re.astype(BF16)
                st[r0:r0 + batch, c_im:c_im + LANES] = n_im.astype(BF16)
                state[2 * q], state[2 * q + 1] = n_re, n_im
        for r in range(2 * PAIRS_PER_BLOCK):
            carry_ref[j, r] = state[r]

        yt = _dot(st[...], wc_ref[j])
        y = jnp.swapaxes(yt.reshape(steps, batch, LANES), 0, 1).reshape(rows, LANES)
        y = y + dskip_ref[:, j * LANES:(j + 1) * LANES] * hn_ref[j]
        z_ref[:, j * LANES:(j + 1) * LANES] = jax.nn.gelu(y).astype(BF16)

    glu = _dot(z_ref[...], wglu_ref[...])
    out = x_ref[...].reshape(rows, d) + glu[:, :d] * jax.nn.sigmoid(glu[:, d:])
    o_ref[...] = out.reshape(batch, steps, d)


def _s5_mixer(x3, norm_rows, g_row, wb, wc, abar_re, abar_im, d_skip, w_glu, j):
    batch, seq, d = x3.shape
    steps = S5_STEPS
    assert seq % steps == 0 and steps % SUBLANES == 0 and batch % (2 * SUBLANES) == 0
    rows = batch * steps
    nblk = d // LANES
    return pl.pallas_call(
        _s5_kernel,
        out_shape=jax.ShapeDtypeStruct((batch, seq, d), F32),
        grid=(seq // steps,),
        in_specs=[
            pl.BlockSpec((batch, steps, d), lambda c: (0, c, 0)),
            pl.BlockSpec((None, 1, d), lambda c: (g_row, 0, 0)),
            _resident((None, nblk, LANES, STATE_COLS), lambda c: (j, 0, 0, 0)),
            _resident((None, nblk, STATE_COLS, LANES), lambda c: (j, 0, 0, 0)),
            _resident((None, nblk * PAIRS_PER_BLOCK, LANES), lambda c: (j, 0, 0)),
            _resident((None, nblk * PAIRS_PER_BLOCK, LANES), lambda c: (j, 0, 0)),
            pl.BlockSpec((None, 1, d), lambda c: (j, 0, 0)),
            _resident((d, 2 * d), lambda c: (0, 0)),
        ],
        out_specs=pl.BlockSpec((batch, steps, d), lambda c: (0, c, 0)),
        scratch_shapes=[
            pltpu.VMEM((nblk, rows, LANES), F32),
            pltpu.VMEM((2, rows, STATE_COLS), F32),
            pltpu.VMEM((2, rows, STATE_COLS), BF16),
            pltpu.VMEM((rows, d), BF16),
            pltpu.VMEM((nblk, 2 * PAIRS_PER_BLOCK, batch, LANES), F32),
        ],
        compiler_params=_params(("arbitrary",)),
        name="s5_mixer",
    )(x3, norm_rows, wb, wc, abar_re, abar_im, d_skip, w_glu)


def kernel(x, mem, norm_g, final_g, ffn1_up, ffn1_down, ffn2_up, ffn2_down,
           conv_w_in, conv_w, conv_w_out,
           ssm_a_re, ssm_a_im, ssm_log_dt, ssm_b_re, ssm_b_im,
           ssm_c_re, ssm_c_im, ssm_d, ssm_w_glu,
           xa_w_q, xa_w_kv, xa_w_o):
    batch, seq, d = x.shape
    depth = norm_g.shape[0]
    assert norm_g.shape[1] == N_NORMS
    assert ssm_b_re.shape[-1] == SSM_GROUP and ssm_a_re.shape[-1] == SSM_STATE

    norm_rows = norm_g.reshape(depth * N_NORMS, 1, d)
    final_row = final_g.reshape(1, d)
    s5 = _s5_prepare(ssm_a_re, ssm_a_im, ssm_log_dt, ssm_b_re, ssm_b_im, ssm_c_re, ssm_c_im)
    s5_d = ssm_d.reshape(ssm_d.shape[0], 1, d)

    ffn1_w = (ffn1_up, ffn1_down)
    for i in range(depth):
        row = i * N_NORMS
        j = i // 2
        mixer_w = ((conv_w_in, j), (conv_w_out, j)) if i % 2 == 0 else ((ssm_w_glu, j),)
        x, w_bf = _ffn(x, norm_rows, row + 0, *ffn1_w, i, final_row, False,
                       cast=mixer_w + ((xa_w_kv, i), (xa_w_q, i), (xa_w_o, i),
                                       (ffn2_up, i), (ffn2_down, i)))
        w_kv, w_q, w_o, up2, down2 = w_bf[-5:]
        if i % 2 == 0:
            x = _conv_mixer(x, norm_rows, row + 1, w_bf[0], conv_w, w_bf[1], j)
        else:
            x = _s5_mixer(x, norm_rows, row + 1, *s5, s5_d, w_bf[0], j)
        kt, v = _memory_kv(mem, norm_rows, row + 3, w_kv)
        x = _cross_attention(x, norm_rows, row + 2, w_q, kt, v, w_o)
        last = i == depth - 1
        x, ffn1_w = _ffn(x, norm_rows, row + 4, up2, down2, i, final_row, last,
                         cast=() if last else ((ffn1_up, i + 1), (ffn1_down, i + 1)))
    return x
```
